```python
import math
import jax, jax.numpy as jnp
from jax import lax
import numpy as np

D_MODEL = 4096
BATCH = 32
SEQ = 256
DEPTH = 4
DEC_BATCH = 8
DEC_SEQ = 4096
PAST_LEN = 512

GRID_W = 64
N_EVEN = (DEPTH + 1) // 2
N_ODD = DEPTH // 2
RET_H = 8
RET_DK = 256
RET_DV = 256
GLA_H = 4
GLA_DK = 256
GLA_DV = 512
GLA_RANK = 16
GLA_TAU = 16.0
DA_H = 16
DA_DH = 128
D_FF = 11008
CONV_W = 3
CHUNK = 64
Q_BLOCK = 128
ROPE_BASE = 10000.0
EPS = 1e-6

RET_QKW = RET_H * RET_DK
RET_VW = RET_H * RET_DV
GLA_KW = GLA_H * GLA_DK
GLA_VW = GLA_H * GLA_DV
EVEN_SPLITS = (RET_QKW, RET_QKW, RET_VW, RET_VW, GLA_KW, GLA_KW, GLA_VW, GLA_VW, GLA_RANK, GLA_RANK)
EVEN_IN = 2 * RET_QKW + 2 * RET_VW + 2 * GLA_KW + 2 * GLA_VW + 2 * GLA_RANK
EVEN_OUT = RET_VW + GLA_VW
ODD_IN = 3 * DA_H * 2 * DA_DH
ODD_OUT = DA_H * 2 * DA_DH

kernel_name = "hybrid_diffusion_retention_gla_diffattn_step"


def rmsnorm(x, g):
    xf = x.astype(jnp.float32)
    xf = xf * lax.rsqrt(jnp.mean(xf * xf, axis=-1, keepdims=True) + EPS)
    return xf.astype(x.dtype) * g


def head_rmsnorm(o, g):
    B, T, H, V = o.shape
    of = o.astype(jnp.float32)
    of = of * lax.rsqrt(jnp.mean(of * of, axis=-1, keepdims=True) + EPS)
    return of.reshape(B, T, H * V).astype(o.dtype) * g


def modulate(x, g, shift, scale):
    return rmsnorm(x, g) * (1.0 + scale[:, None]) + shift[:, None]


def split_cols(z, sizes):
    idx, acc = [], 0
    for s in sizes[:-1]:
        acc += s
        idx.append(acc)
    return jnp.split(z, idx, axis=-1)


def axial_rope_tables(n_tokens, head_dim):
    n_rows = n_tokens // GRID_W
    row = jnp.repeat(jnp.arange(n_rows), GRID_W).astype(jnp.float32)
    col = jnp.tile(jnp.arange(GRID_W), n_rows).astype(jnp.float32)
    quarter = head_dim // 4
    inv = ROPE_BASE ** (-jnp.arange(quarter, dtype=jnp.float32) / quarter)
    ang = jnp.concatenate([row[:, None] * inv, col[:, None] * inv], axis=-1)
    return jnp.cos(ang), jnp.sin(ang)


def apply_rope(x, cos, sin):
    half = x.shape[-1] // 2
    shape = (1, cos.shape[0]) + (1,) * (x.ndim - 3) + (half,)
    c = cos.reshape(shape).astype(x.dtype)
    s = sin.reshape(shape).astype(x.dtype)
    x1, x2 = x[..., :half], x[..., half:]
    return jnp.concatenate([x1 * c - x2 * s, x1 * s + x2 * c], axis=-1)


def chunk_scan(q, k, v, log_a, s0):
    dt = q.dtype
    B, T, H, _ = q.shape
    n = T // CHUNK

    def split(a):
        a = a.astype(jnp.float32)
        return a.reshape((a.shape[0], n, CHUNK) + a.shape[2:]).swapaxes(0, 1)

    lower = jnp.tril(jnp.ones((CHUNK, CHUNK), dtype=bool))

    def step(s, xs):
        qc, kc, vc, lc = xs
        b = jnp.cumsum(lc, axis=1)
        b_last = b[:, -1:]
        q_in = qc * jnp.exp(b)
        k_in = kc * jnp.exp(-b)
        k_out = kc * jnp.exp(b_last - b)
        att = jnp.where(lower, jnp.einsum('bthk,bshk->bhts', q_in, k_in), 0.0)
        o = jnp.einsum('bhts,bshv->bthv', att, vc) + jnp.einsum('bthk,bhkv->bthv', q_in, s)
        s = jnp.exp(b_last[:, 0])[..., None] * s + jnp.einsum('bshk,bshv->bhkv', k_out, vc)
        return s, o

    s_fin, o = lax.scan(step, s0.astype(jnp.float32), (split(q), split(k), split(v), split(log_a)))
    return o.swapaxes(0, 1).reshape(B, T, H, -1).astype(dt), s_fin.astype(dt)


def bidir_scan(q, k, v, la_f, la_b, s0_f, s0_b):
    flip = lambda a: jnp.flip(a, axis=1)
    o_f, s_f = chunk_scan(q, k, v, la_f, s0_f)
    o_b, s_b = chunk_scan(flip(q), flip(k), flip(v), flip(la_b), s0_b)
    return o_f + flip(o_b), s_f, s_b


def even_mixer(h, w_in, w_out, ret_decay, ret_norm_g, gla_w2, gla_b, gla_norm_g, rope, s0):
    B, T, _ = h.shape
    rq, rk, rv, rg, gq, gk, gv, gr, gf1, gb1 = split_cols(h @ w_in, EVEN_SPLITS)
    rq = rq.reshape(B, T, RET_H, RET_DK)
    rk = rk.reshape(B, T, RET_H, RET_DK) * (RET_DK ** -0.5)
    rv = rv.reshape(B, T, RET_H, RET_DV)
    if rope is not None:
        rq = apply_rope(rq, rope[0], rope[1])
        rk = apply_rope(rk, rope[0], rope[1])
    log_g = -jnp.exp(ret_decay.astype(jnp.float32))
    la = lambda d: jnp.broadcast_to(log_g[d][None, None, :, None], (1, T, RET_H, 1))
    ro, rs_f, rs_b = bidir_scan(rq, rk, rv, la(0), la(1), s0[0], s0[1])
    ro = head_rmsnorm(ro, ret_norm_g) * jax.nn.silu(rg)
    gq = gq.reshape(B, T, GLA_H, GLA_DK) * (GLA_DK ** -0.5)
    gk = gk.reshape(B, T, GLA_H, GLA_DK)
    gv = gv.reshape(B, T, GLA_H, GLA_DV)

    def gla_log_a(r, d):
        logits = (r @ gla_w2[d] + gla_b[d]).astype(jnp.float32)
        return (jax.nn.log_sigmoid(logits) / GLA_TAU).reshape(B, T, GLA_H, GLA_DK)

    go, gs_f, gs_b = bidir_scan(gq, gk, gv, gla_log_a(gf1, 0), gla_log_a(gb1, 1), s0[2], s0[3])
    go = head_rmsnorm(go, gla_norm_g) * jax.nn.silu(gr)
    out = jnp.concatenate([ro, go], axis=-1) @ w_out
    return out, (rs_f, rs_b, gs_f, gs_b)


def diff_attention(q, k, v, lam):
    B, T, H, _, Dh = q.shape
    nb = T // Q_BLOCK
    qb = q.reshape(B, nb, Q_BLOCK, H, 2, Dh).swapaxes(0, 1)
    scale = Dh ** -0.5

    def one(qblk):
        s = jnp.einsum('bqhmd,bkhmd->bhmqk', qblk, k, preferred_element_type=jnp.float32) * scale
        p = jax.nn.softmax(s, axis=-1)
        a = p[:, :, 0] - lam * p[:, :, 1]
        return jnp.einsum('bhqk,bkhe->bqhe', a.astype(v.dtype), v)

    o = lax.map(one, qb)
    return o.swapaxes(0, 1).reshape(B, T, H, 2 * Dh)


def odd_mixer(h, w_in, w_out, da_lambda, da_subln_g, lam_init, rope, ctx):
    B, T, _ = h.shape
    q, k, v = jnp.split(h @ w_in, 3, axis=-1)
    k_store = k.reshape(B, T, DA_H, 2 * DA_DH)
    q = q.reshape(B, T, DA_H, 2, DA_DH)
    k = k.reshape(B, T, DA_H, 2, DA_DH)
    v = v.reshape(B, T, DA_H, 2 * DA_DH)
    if rope is not None:
        q = apply_rope(q, rope[0], rope[1])
        k = apply_rope(k, rope[0], rope[1])
    lq1, lk1, lq2, lk2 = da_lambda[0], da_lambda[1], da_lambda[2], da_lambda[3]
    lam = (jnp.exp(jnp.sum(lq1 * lk1).astype(jnp.float32))
           - jnp.exp(jnp.sum(lq2 * lk2).astype(jnp.float32)) + lam_init)
    if ctx is None:
        k_all, v_all = k, v
    else:
        ck, cv = ctx
        k_all = jnp.concatenate([k, ck.reshape(B, ck.shape[1], DA_H, 2, DA_DH)], axis=1)
        v_all = jnp.concatenate([v, cv], axis=1)
    o = diff_attention(q, k_all, v_all, lam)
    o = rmsnorm(o, da_subln_g) * (1.0 - lam_init)
    return o.reshape(B, T, ODD_OUT) @ w_out, k_store, v


def conv_ffn(h, w_in, conv_w, conv_b, w_out):
    def per_seq(hs):
        u = hs @ w_in
        up = jnp.pad(u, ((1, 1), (0, 0)))
        u = up[:-2] * conv_w[0] + up[1:-1] * conv_w[1] + up[2:] * conv_w[2] + conv_b
        g, val = jnp.split(u, 2, axis=-1)
        return (jax.nn.silu(g) * val) @ w_out
    return lax.map(per_seq, h)


def setup_inputs(seed: int = 0) -> dict:
    key = jax.random.key(seed)
    ks = jax.random.split(key, 26)
    f32 = jnp.float32
    nrm = lambda k, shape, s=1.0: jax.random.normal(k, shape, f32) * s
    base_decay = jnp.asarray(np.log(-np.log(1.0 - 2.0 ** (-5.0 - np.arange(RET_H)))).astype(np.float32))
    conv_centre = jnp.asarray(np.array([0.0, 1.0, 0.0], dtype=np.float32))[None, :, None]
    return {
        "x_prompt": nrm(ks[0], (BATCH, SEQ, D_MODEL)),
        "x_sample": nrm(ks[1], (DEC_BATCH, DEC_SEQ, D_MODEL)),
        "state_ret": nrm(ks[2], (DEC_BATCH, N_EVEN, 2, RET_H, RET_DK, RET_DV), 0.05),
        "state_gla": nrm(ks[3], (DEC_BATCH, N_EVEN, 2, GLA_H, GLA_DK, GLA_DV), 0.05),
        "cache_k": nrm(ks[4], (DEC_BATCH, N_ODD, PAST_LEN, DA_H, 2 * DA_DH)),
        "cache_v": nrm(ks[5], (DEC_BATCH, N_ODD, PAST_LEN, DA_H, 2 * DA_DH)),
        "c": nrm(ks[6], (DEC_BATCH, D_MODEL)),
        "c_ctx": nrm(ks[7], (D_MODEL,)),
        "w_mod": nrm(ks[8], (DEPTH, D_MODEL, 6 * D_MODEL), 0.5 * D_MODEL ** -0.5),
        "b_mod": nrm(ks[9], (DEPTH, 6 * D_MODEL), 0.02),
        "norm_g": 1.0 + nrm(ks[10], (DEPTH, 4, D_MODEL), 0.02),
        "w_even_in": nrm(ks[11], (N_EVEN, D_MODEL, EVEN_IN), D_MODEL ** -0.5),
        "w_even_out": nrm(ks[12], (N_EVEN, EVEN_OUT, D_MODEL), EVEN_OUT ** -0.5),
        "ret_decay": base_decay[None, None, :] + nrm(ks[13], (N_EVEN, 2, RET_H), 0.1),
        "ret_norm_g": 1.0 + nrm(ks[14], (N_EVEN, RET_VW), 0.02),
        "gla_w2": nrm(ks[15], (N_EVEN, 2, GLA_RANK, GLA_KW), GLA_RANK ** -0.5),
        "gla_b": nrm(ks[16], (N_EVEN, 2, GLA_KW), 0.02),
        "gla_norm_g": 1.0 + nrm(ks[17], (N_EVEN, GLA_VW), 0.02),
        "w_odd_in": nrm(ks[18], (N_ODD, D_MODEL, ODD_IN), D_MODEL ** -0.5),
        "w_odd_out": nrm(ks[19], (N_ODD, ODD_OUT, D_MODEL), ODD_OUT ** -0.5),
        "da_lambda": nrm(ks[20], (N_ODD, 4, DA_DH), 0.1),
        "da_subln_g": 1.0 + nrm(ks[21], (N_ODD, 2 * DA_DH), 0.02),
        "w_ffn_in": nrm(ks[22], (DEPTH, D_MODEL, 2 * D_FF), D_MODEL ** -0.5),
        "ffn_conv_w": conv_centre + nrm(ks[23], (DEPTH, CONV_W, 2 * D_FF), 0.3),
        "ffn_conv_b": nrm(ks[24], (DEPTH, 2 * D_FF), 0.02),
        "w_ffn_out": nrm(ks[25], (DEPTH, D_FF, D_MODEL), D_FF ** -0.5),
    }


def reference(x_prompt, x_sample, state_ret, state_gla, cache_k, cache_v, c, c_ctx, w_mod, b_mod, norm_g,
              w_even_in, w_even_out, ret_decay, ret_norm_g, gla_w2, gla_b, gla_norm_g,
              w_odd_in, w_odd_out, da_lambda, da_subln_g, w_ffn_in, ffn_conv_w, ffn_conv_b, w_ffn_out):
    Bp = x_prompt.shape[0]
    T_lat = x_sample.shape[1]
    ret_rope = axial_rope_tables(T_lat, RET_DK)
    da_rope = axial_rope_tables(T_lat, DA_DH)
    zeros_ctx = (jnp.zeros((Bp, RET_H, RET_DK, RET_DV), jnp.float32),
                 jnp.zeros((Bp, RET_H, RET_DK, RET_DV), jnp.float32),
                 jnp.zeros((Bp, GLA_H, GLA_DK, GLA_DV), jnp.float32),
                 jnp.zeros((Bp, GLA_H, GLA_DK, GLA_DV), jnp.float32))

    yp, ys = x_prompt, x_sample
    ret_states, gla_states, ks_out, vs_out = [], [], [], []
    for l in range(DEPTH):
        mp = jnp.split(jax.nn.silu(c_ctx)[None] @ w_mod[l] + b_mod[l], 6, axis=-1)
        ms = jnp.split(jax.nn.silu(c) @ w_mod[l] + b_mod[l], 6, axis=-1)
        hp = modulate(yp, norm_g[l, 0], mp[0], mp[1])
        hs = modulate(ys, norm_g[l, 0], ms[0], ms[1])
        if l % 2 == 0:
            e = l // 2
            prm = (w_even_in[e], w_even_out[e], ret_decay[e], ret_norm_g[e], gla_w2[e], gla_b[e], gla_norm_g[e])
            op, (rf, rb, gf, gb) = even_mixer(hp, *prm, None, zeros_ctx)
            s0_lat = (state_ret[:, e, 0], state_ret[:, e, 1], state_gla[:, e, 0], state_gla[:, e, 1])
            os_, _ = even_mixer(hs, *prm, ret_rope, s0_lat)
            ret_states.append(jnp.stack([rf, rb], axis=1))
            gla_states.append(jnp.stack([gf, gb], axis=1))
        else:
            o = l // 2
            lam_init = 0.8 - 0.6 * math.exp(-0.3 * l)
            prm = (w_odd_in[o], w_odd_out[o], da_lambda[o], da_subln_g[o], lam_init)
            op, kp, vp = odd_mixer(hp, *prm, None, None)
            os_, _, _ = odd_mixer(hs, *prm, da_rope, (cache_k[:, o], cache_v[:, o]))
            ks_out.append(kp)
            vs_out.append(vp)
        yp = yp + mp[2][:, None] * rmsnorm(op, norm_g[l, 1])
        ys = ys + ms[2][:, None] * rmsnorm(os_, norm_g[l, 1])
        hp = modulate(yp, norm_g[l, 2], mp[3], mp[4])
        hs = modulate(ys, norm_g[l, 2], ms[3], ms[4])
        fp = conv_ffn(hp, w_ffn_in[l], ffn_conv_w[l], ffn_conv_b[l], w_ffn_out[l])
        fs = conv_ffn(hs, w_ffn_in[l], ffn_conv_w[l], ffn_conv_b[l], w_ffn_out[l])
        yp = yp + mp[5][:, None] * rmsnorm(fp, norm_g[l, 3])
        ys = ys + ms[5][:, None] * rmsnorm(fs, norm_g[l, 3])

    new_state_ret = jnp.stack(ret_states, axis=1)
    new_state_gla = jnp.stack(gla_states, axis=1)
    new_cache_k = jnp.stack(ks_out, axis=1)
    new_cache_v = jnp.stack(vs_out, axis=1)
    return (yp, ys, new_state_ret, new_state_gla, new_cache_k, new_cache_v)
```

```python
import functools
import math

import jax
import jax.numpy as jnp
from jax import lax
from jax.experimental import pallas as pl
from jax.experimental.pallas import tpu as pltpu

F32 = jnp.float32
BF16 = jnp.bfloat16

V7X_VMEM_LIMIT_BYTES = 58 * 1024 * 1024
LANES = 128
HALO = 8

GRID_W = 64
RET_H, RET_DK, RET_DV = 8, 256, 256
GLA_H, GLA_DK, GLA_DV = 4, 256, 512
GLA_RANK = 16
GLA_TAU = 16.0
DA_H, DA_DH = 16, 128
CHUNK = 64
ROPE_BASE = 10000.0
EPS = 1e-6
N_MOD = 6
MOD_ROWS = 16


def _cparams(sem):
    return pltpu.CompilerParams(dimension_semantics=sem, vmem_limit_bytes=V7X_VMEM_LIMIT_BYTES)


def _dot(a, b):
    return jnp.dot(a, b, preferred_element_type=F32)


def _dot_nt(a, b):
    return lax.dot_general(a, b, (((1,), (1,)), ((), ())), preferred_element_type=F32)


def _dot_tn(a, b):
    return lax.dot_general(a, b, (((0,), (0,)), ((), ())), preferred_element_type=F32)


def _silu(x):
    return x * (1.0 / (1.0 + jnp.exp(-x)))


def _mod_kernel(c_ref, w_ref, b_ref, o_ref):
    s = _silu(c_ref[...]).astype(BF16)
    o_ref[...] = _dot(s, w_ref[...].astype(BF16)) + b_ref[...]


def mod_matmul(cond, w_mod, b_mod, tn=512):
    L, D, N = w_mod.shape
    return pl.pallas_call(
        _mod_kernel,
        grid=(L, N // tn),
        in_specs=[
            pl.BlockSpec((MOD_ROWS, D), lambda l, j: (0, 0)),
            pl.BlockSpec((None, D, tn), lambda l, j: (l, 0, j)),
            pl.BlockSpec((None, 1, tn), lambda l, j: (l, 0, j)),
        ],
        out_specs=pl.BlockSpec((None, MOD_ROWS, tn), lambda l, j: (l, 0, j)),
        out_shape=jax.ShapeDtypeStruct((L, MOD_ROWS, N), F32),
        compiler_params=_cparams(("arbitrary", "arbitrary")),
        name="mod_matmul",
    )(cond, w_mod, b_mod.reshape(L, 1, N))


def _rms(x):
    return x * lax.rsqrt(jnp.mean(x * x, axis=-1, keepdims=True) + EPS)


def _rowwise_kernel(*refs, has_resid, has_mod):
    refs = list(refs)
    y_ref = refs.pop(0)
    y = y_ref[...]
    if has_resid:
        o_ref, gpost_ref, gate_ref = refs.pop(0), refs.pop(0), refs.pop(0)
        y = y + gate_ref[...] * (_rms(o_ref[...]) * gpost_ref[...])
    if has_mod:
        gpre_ref, shift_ref, scale_ref = refs.pop(0), refs.pop(0), refs.pop(0)
    if has_resid:
        ynew_ref = refs.pop(0)
        ynew_ref[...] = y
    if has_mod:
        h_ref = refs.pop(0)
        h = (_rms(y) * gpre_ref[...]) * (1.0 + scale_ref[...]) + shift_ref[...]
        h_ref[...] = h.astype(BF16)


def rowwise(y, mods, group0, rows_per_group, *, resid=None, mod=None, tr=256):
    M, D = y.shape
    tr = min(tr, M)

    def mod_spec(layer, chunk):
        return pl.BlockSpec((None, None, 1, D),
                            lambda i: (layer, group0 + (i * tr) // rows_per_group, 0, chunk))

    row_spec = pl.BlockSpec((tr, D), lambda i: (i, 0))
    vec_spec = pl.BlockSpec((1, D), lambda i: (0, 0))
    args, in_specs = [y], [row_spec]
    out_shape, out_specs = [], []
    if resid is not None:
        o, g_post, layer, gate_chunk = resid
        args += [o, g_post.reshape(1, D), mods]
        in_specs += [row_spec, vec_spec, mod_spec(layer, gate_chunk)]
        out_shape.append(jax.ShapeDtypeStruct((M, D), F32))
        out_specs.append(row_spec)
    if mod is not None:
        g_pre, layer, shift_chunk, scale_chunk = mod
        args += [g_pre.reshape(1, D), mods, mods]
        in_specs += [vec_spec, mod_spec(layer, shift_chunk), mod_spec(layer, scale_chunk)]
        out_shape.append(jax.ShapeDtypeStruct((M, D), BF16))
        out_specs.append(row_spec)
    return pl.pallas_call(
        functools.partial(_rowwise_kernel, has_resid=resid is not None, has_mod=mod is not None),
        grid=(M // tr,),
        in_specs=in_specs,
        out_specs=out_specs,
        out_shape=out_shape,
        compiler_params=_cparams(("arbitrary",)),
        name="rowwise",
    )(*args)


def _mm_kernel(x_ref, w_ref, o_ref):
    o_ref[...] = _dot(x_ref[...], w_ref[...]).astype(o_ref.dtype)


def _mm_acc_kernel(x_ref, w_ref, o_ref, acc_ref):
    k = pl.program_id(2)

    @pl.when(k == 0)
    def _():
        acc_ref[...] = jnp.zeros_like(acc_ref)

    acc_ref[...] += _dot(x_ref[...], w_ref[...])

    @pl.when(k == pl.num_programs(2) - 1)
    def _():
        o_ref[...] = acc_ref[...].astype(o_ref.dtype)


def matmul(x, w, out_dtype, tm=1024, tn=1024, tk=None):
    M, K = x.shape
    N = w.shape[1]
    tm, tn = min(tm, M), min(tn, N)
    if tk is None or tk >= K:
        return pl.pallas_call(
            _mm_kernel,
            grid=(M // tm, N // tn),
            in_specs=[pl.BlockSpec((tm, K), lambda i, j: (i, 0)),
                      pl.BlockSpec((K, tn), lambda i, j: (0, j))],
            out_specs=pl.BlockSpec((tm, tn), lambda i, j: (i, j)),
            out_shape=jax.ShapeDtypeStruct((M, N), out_dtype),
            compiler_params=_cparams(("arbitrary", "arbitrary")),
            name="matmul",
        )(x, w)
    return pl.pallas_call(
        _mm_acc_kernel,
        grid=(M // tm, N // tn, K // tk),
        in_specs=[pl.BlockSpec((tm, tk), lambda i, j, k: (i, k)),
                  pl.BlockSpec((tk, tn), lambda i, j, k: (k, j))],
        out_specs=pl.BlockSpec((tm, tn), lambda i, j, k: (i, j)),
        out_shape=jax.ShapeDtypeStruct((M, N), out_dtype),
        scratch_shapes=[pltpu.VMEM((tm, tn), F32)],
        compiler_params=_cparams(("arbitrary", "arbitrary", "arbitrary")),
        name="matmul_kgrid",
    )(x, w)


def _ffn_in_kernel(h_ref, hprev_ref, hnext_ref, wg_ref, wv_ref, cw_ref, cb_ref, o_ref, hext_ref,
                   *, tm, seq_len):
    i = pl.program_id(0)

    @pl.when(pl.program_id(1) == 0)
    def _():
        hext_ref[0:HALO, :] = hprev_ref[...]
        hext_ref[HALO:HALO + tm, :] = h_ref[...]
        hext_ref[HALO + tm:, :] = hnext_ref[...]

    hext = hext_ref[...]
    rows = tm + 2 * HALO
    pos = (i * tm + lax.broadcasted_iota(jnp.int32, (tm, 1), 0)) % seq_len
    not_first = (pos != 0).astype(F32)
    not_last = (pos != seq_len - 1).astype(F32)

    def conv(w_ref, half):
        u = _dot(hext, w_ref[...])
        u_prev = pltpu.roll(u, 1, axis=0)[HALO:HALO + tm]
        u_next = pltpu.roll(u, rows - 1, axis=0)[HALO:HALO + tm]
        cw = cw_ref[half]
        return (u_prev * not_first * cw[0:1] + u[HALO:HALO + tm] * cw[1:2]
                + u_next * not_last * cw[2:3] + cb_ref[half])

    g = conv(wg_ref, 0)
    val = conv(wv_ref, 1)
    o_ref[...] = (_silu(g) * val).astype(o_ref.dtype)


def ffn_in(h, wg, wv, cw, cb, seq_len, tm=1024, tf=256):
    M, D = h.shape
    Fp = wg.shape[1]
    tm, tf = min(tm, M), min(tf, Fp)
    nb = tm // HALO
    last = M // HALO - 1
    return pl.pallas_call(
        functools.partial(_ffn_in_kernel, tm=tm, seq_len=seq_len),
        grid=(M // tm, Fp // tf),
        in_specs=[
            pl.BlockSpec((tm, D), lambda i, j: (i, 0)),
            pl.BlockSpec((HALO, D), lambda i, j: (jnp.maximum(i * nb - 1, 0), 0)),
            pl.BlockSpec((HALO, D), lambda i, j: (jnp.minimum((i + 1) * nb, last), 0)),
            pl.BlockSpec((D, tf), lambda i, j: (0, j)),
            pl.BlockSpec((D, tf), lambda i, j: (0, j)),
            pl.BlockSpec((2, 3, tf), lambda i, j: (0, 0, j)),
            pl.BlockSpec((2, 1, tf), lambda i, j: (0, 0, j)),
        ],
        out_specs=pl.BlockSpec((tm, tf), lambda i, j: (i, j)),
        out_shape=jax.ShapeDtypeStruct((M, Fp), BF16),
        scratch_shapes=[pltpu.VMEM((tm + 2 * HALO, D), BF16)],
        compiler_params=_cparams(("arbitrary", "arbitrary")),
        name="ffn_in",
    )(h, h, h, wg, wv, cw, cb)


def _scan_kernel(*refs, T, K, V, is_gla, use_rope, has_s0, emit_state, q_scale, k_scale):
    refs = list(refs)
    q_ref, k_ref, v_ref, gate_ref, ng_ref = (refs.pop(0) for _ in range(5))
    if is_gla:
        r_ref, w2_ref, gb_ref = (refs.pop(0) for _ in range(3))
    else:
        dec_ref = refs.pop(0)
    if use_rope:
        cos_ref, sin_ref = refs.pop(0), refs.pop(0)
    if has_s0:
        s0_ref = refs.pop(0)
    out_ref = refs.pop(0)
    if emit_state:
        sfin_ref = refs.pop(0)
    s_ref, osum_ref = refs.pop(0), refs.pop(0)

    C = CHUNK
    n = T // C
    half = K // 2
    ti = lax.broadcasted_iota(jnp.int32, (C, C), 0)
    si = lax.broadcasted_iota(jnp.int32, (C, C), 1)
    tri = (ti >= si, si >= ti)
    tcol = lax.broadcasted_iota(jnp.int32, (C, 1), 0).astype(F32)

    for d in range(2):
        if has_s0:
            s_ref[d] = s0_ref[d]
        else:
            s_ref[d] = jnp.zeros((K, V), F32)

    if is_gla:
        cum = tuple(m.astype(F32) for m in tri)
    else:
        lg = tuple(-jnp.exp(dec_ref[d][0:1, 0:1]) for d in range(2))
        dmat = (jnp.where(tri[0], jnp.exp((ti - si).astype(F32) * lg[0]), 0.0),
                jnp.where(tri[1], jnp.exp((si - ti).astype(F32) * lg[1]), 0.0))
        e_in = (jnp.exp((tcol + 1.0) * lg[0]), jnp.exp((C - tcol) * lg[1]))
        e_out = (jnp.exp((C - 1.0 - tcol) * lg[0]), jnp.exp(tcol * lg[1]))
        e_tot = (jnp.exp(C * lg[0]), jnp.exp(C * lg[1]))

    def load_qk(ref, rows, scale):
        x = ref[rows, :].astype(F32)
        if use_rope:
            cs, sn = cos_ref[rows, :], sin_ref[rows, :]
            x1, x2 = x[:, :half], x[:, half:]
            x = jnp.concatenate([x1 * cs - x2 * sn, x1 * sn + x2 * cs], axis=-1)
        if scale != 1.0:
            x = x * scale
        return x

    def chunk_step(c, d):
        cc = c if d == 0 else n - 1 - c
        rows = pl.ds(pl.multiple_of(cc * C, C), C)
        qc = load_qk(q_ref, rows, q_scale)
        kc = load_qk(k_ref, rows, k_scale)
        vc = v_ref[rows, :]
        s_old = s_ref[d]
        if is_gla:
            logits = _dot(r_ref[rows, :], w2_ref[d]) + gb_ref[d]
            la = (jnp.minimum(logits, 0.0) - jnp.log(1.0 + jnp.exp(-jnp.abs(logits)))) * (1.0 / GLA_TAU)
            b = jnp.dot(cum[d], la, preferred_element_type=F32, precision=lax.Precision.HIGHEST)
            b_end = b[C - 1:C, :] if d == 0 else b[0:1, :]
            q_in = (qc * jnp.exp(b)).astype(BF16)
            k_in = (kc * jnp.exp(-b)).astype(BF16)
            k_out = kc * jnp.exp(b_end - b)
            att = jnp.where(tri[d], _dot_nt(q_in, k_in), 0.0)
            s_decay = jnp.exp(jnp.sum(la.T, axis=1, keepdims=True))
        else:
            att = _dot_nt(qc.astype(BF16), kc.astype(BF16)) * dmat[d]
            q_in = (qc * e_in[d]).astype(BF16)
            k_out = kc * e_out[d]
            s_decay = e_tot[d]
        o = _dot(att.astype(BF16), vc) + _dot(q_in, s_old.astype(BF16))
        s_ref[d] = s_decay * s_old + _dot(k_out.T.astype(BF16), vc)
        return rows, o

    def first_half(c, carry):
        for d in range(2):
            rows, o = chunk_step(c, d)
            osum_ref[rows, :] = o
        return carry

    def second_half(c, carry):
        for d in range(2):
            rows, o = chunk_step(c, d)
            o = o + osum_ref[rows, :]
            gate = gate_ref[rows, :].astype(F32)
            out_ref[rows, :] = (_rms(o) * ng_ref[...] * _silu(gate)).astype(out_ref.dtype)
        return carry

    lax.fori_loop(0, n // 2, first_half, 0)
    lax.fori_loop(n // 2, n, second_half, 0)
    if emit_state:
        for d in range(2):
            sfin_ref[d] = s_ref[d]


def scan_mixer(z, zr, *, T, col_q, col_k, col_v, col_gate, H, K, V, norm_g, is_gla,
               dec=None, w2=None, gbias=None, rope=None, s0=None, emit_state=False):
    M = z.shape[0]
    B = M // T
    args = [z, z, z, z, norm_g.reshape(1, H * V)]
    in_specs = [
        pl.BlockSpec((T, K), lambda b, h: (b, col_q // K + h)),
        pl.BlockSpec((T, K), lambda b, h: (b, col_k // K + h)),
        pl.BlockSpec((T, V), lambda b, h: (b, col_v // V + h)),
        pl.BlockSpec((T, V), lambda b, h: (b, col_gate // V + h)),
        pl.BlockSpec((1, V), lambda b, h: (0, h)),
    ]
    if is_gla:
        args += [zr, w2, gbias]
        in_specs += [pl.BlockSpec((T, LANES), lambda b, h: (b, 0)),
                     pl.BlockSpec((None, 2, LANES, K), lambda b, h: (h, 0, 0, 0)),
                     pl.BlockSpec((None, 2, 1, K), lambda b, h: (h, 0, 0, 0))]
    else:
        args += [dec]
        in_specs += [pl.BlockSpec((None, 2, 8, LANES), lambda b, h: (h, 0, 0, 0))]
    if rope is not None:
        args += [rope[0], rope[1]]
        in_specs += [pl.BlockSpec((T, K // 2), lambda b, h: (0, 0))] * 2
    if s0 is not None:
        s0_arr, s0_e = s0
        args += [s0_arr]
        in_specs += [pl.BlockSpec((None, None, 2, None, K, V), lambda b, h: (b, s0_e, 0, h, 0, 0))]
    out_shape = [jax.ShapeDtypeStruct((M, H * V), BF16)]
    out_specs = [pl.BlockSpec((T, V), lambda b, h: (b, h))]
    if emit_state:
        out_shape.append(jax.ShapeDtypeStruct((B, 2, H, K, V), F32))
        out_specs.append(pl.BlockSpec((None, 2, None, K, V), lambda b, h: (b, 0, h, 0, 0)))
    kern = functools.partial(
        _scan_kernel, T=T, K=K, V=V, is_gla=is_gla, use_rope=rope is not None, has_s0=s0 is not None,
        emit_state=emit_state, q_scale=(K ** -0.5 if is_gla else 1.0), k_scale=(1.0 if is_gla else K ** -0.5))
    return pl.pallas_call(
        kern,
        grid=(B, H),
        in_specs=in_specs,
        out_specs=out_specs,
        out_shape=out_shape,
        scratch_shapes=[pltpu.VMEM((2, K, V), F32), pltpu.VMEM((T, V), F32)],
        compiler_params=_cparams(("arbitrary", "arbitrary")),
        name="scan_gla" if is_gla else "scan_ret",
    )(*args)


def _attn_kernel(*refs, T, tq, tk, L, use_rope, lam_init):
    refs = list(refs)
    q_ref, k_ref, v_ref, lam_ref, g_ref = (refs.pop(0) for _ in range(5))
    if L:
        ck_ref, cv_ref = refs.pop(0), refs.pop(0)
    if use_rope:
        cos_ref, sin_ref = refs.pop(0), refs.pop(0)
    o_ref = refs.pop(0)
    if use_rope:
        kr_ref = refs.pop(0)
    Dh = DA_DH
    qi = pl.program_id(2)
    scale = Dh ** -0.5

    def rope(x, rows):
        cs, sn = cos_ref[rows, :], sin_ref[rows, :]
        return jnp.concatenate(
            [x[:, m * Dh:(m + 1) * Dh] * cs + pltpu.roll(x[:, m * Dh:(m + 1) * Dh], Dh // 2, axis=1) * sn
             for m in range(2)], axis=-1)

    if use_rope:
        @pl.when(qi == 0)
        def _():
            def body(j, carry):
                rows = pl.ds(pl.multiple_of(j * tk, tk), tk)
                kr_ref[rows, :] = rope(k_ref[rows, :].astype(F32), rows).astype(BF16)
                return carry
            lax.fori_loop(0, T // tk, body, 0)

    qrows = pl.ds(pl.multiple_of(qi * tq, tq), tq)
    q = q_ref[...].astype(F32)
    if use_rope:
        q = rope(q, qrows)
    q = (q * scale).astype(BF16)
    qs = (q[:, :Dh], q[:, Dh:])

    def update(carry, kblk, vblk):
        new = []
        for m in range(2):
            mx, l, acc = carry[m]
            s = _dot_nt(qs[m], kblk[:, m * Dh:(m + 1) * Dh])
            mx_new = jnp.maximum(mx, jnp.max(s, axis=-1, keepdims=True))
            alpha = jnp.exp(mx - mx_new)
            p = jnp.exp(s - mx_new)
            l = alpha * l + jnp.sum(p, axis=-1, keepdims=True)
            acc = alpha * acc + _dot(p.astype(BF16), vblk)
            new.append((mx_new, l, acc))
        return tuple(new)

    init = tuple((jnp.full((tq, 1), -jnp.inf, F32), jnp.zeros((tq, 1), F32), jnp.zeros((tq, 2 * Dh), F32))
                 for _ in range(2))
    ksrc = kr_ref if use_rope else k_ref

    def body(j, carry):
        rows = pl.ds(pl.multiple_of(j * tk, tk), tk)
        return update(carry, ksrc[rows, :], v_ref[rows, :])

    carry = lax.fori_loop(0, T // tk, body, init)
    if L:
        carry = update(carry, ck_ref[...], cv_ref[...])

    lp = lam_ref[...]
    lam = (jnp.exp(jnp.sum(lp[0:1] * lp[1:2], axis=-1, keepdims=True))
           - jnp.exp(jnp.sum(lp[2:3] * lp[3:4], axis=-1, keepdims=True)) + lam_init)
    (_, l1, a1), (_, l2, a2) = carry
    o = a1 / l1 - lam * (a2 / l2)
    o_ref[...] = (_rms(o) * g_ref[...] * (1.0 - lam_init)).astype(o_ref.dtype)


def diff_attention(z, *, T, lam_p, subln_g, lam_init, ctx=None, rope=None, tq=512, tk=512):
    M = z.shape[0]
    B = M // T
    H, W = DA_H, 2 * DA_DH
    tq, tk = min(tq, T), min(tk, T)
    nq = T // tq
    args = [z, z, z, lam_p, subln_g.reshape(1, W)]
    in_specs = [
        pl.BlockSpec((tq, W), lambda b, h, i: (b * nq + i, h)),
        pl.BlockSpec((T, W), lambda b, h, i: (b, H + h)),
        pl.BlockSpec((T, W), lambda b, h, i: (b, 2 * H + h)),
        pl.BlockSpec((4, DA_DH), lambda b, h, i: (0, 0)),
        pl.BlockSpec((1, W), lambda b, h, i: (0, 0)),
    ]
    L = 0
    if ctx is not None:
        L = ctx[0].shape[1]
        args += [ctx[0], ctx[1]]
        in_specs += [pl.BlockSpec((None, L, W), lambda b, h, i: (b, 0, h))] * 2
    scratch = []
    if rope is not None:
        args += [rope[0], rope[1]]
        in_specs += [pl.BlockSpec((T, DA_DH), lambda b, h, i: (0, 0))] * 2
        scratch.append(pltpu.VMEM((T, W), BF16))
    return pl.pallas_call(
        functools.partial(_attn_kernel, T=T, tq=tq, tk=tk, L=L, use_rope=rope is not None, lam_init=lam_init),
        grid=(B, H, nq),
        in_specs=in_specs,
        out_specs=pl.BlockSpec((tq, W), lambda b, h, i: (b * nq + i, h)),
        out_shape=jax.ShapeDtypeStruct((M, H * W), BF16),
        scratch_shapes=scratch,
        compiler_params=_cparams(("arbitrary", "arbitrary", "arbitrary")),
        name="diff_attention",
    )(*args)


def _rope_tables(n_tokens, head_dim):
    n_rows = n_tokens // GRID_W
    row = jnp.repeat(jnp.arange(n_rows), GRID_W).astype(F32)
    col = jnp.tile(jnp.arange(GRID_W), n_rows).astype(F32)
    quarter = head_dim // 4
    inv = ROPE_BASE ** (-jnp.arange(quarter, dtype=F32) / quarter)
    ang = jnp.concatenate([row[:, None] * inv, col[:, None] * inv], axis=-1)
    return jnp.cos(ang), jnp.sin(ang)


def _pad_cols(a, n):
    return jnp.pad(a, [(0, 0)] * (a.ndim - 1) + [(0, n - a.shape[-1])])


def kernel(x_prompt, x_sample, state_ret, state_gla, cache_k, cache_v, c, c_ctx, w_mod, b_mod, norm_g,
           w_even_in, w_even_out, ret_decay, ret_norm_g, gla_w2, gla_b, gla_norm_g,
           w_odd_in, w_odd_out, da_lambda, da_subln_g, w_ffn_in, ffn_conv_w, ffn_conv_b, w_ffn_out):
    Bp, Tp, D = x_prompt.shape
    Bs, Ts, _ = x_sample.shape
    depth = w_mod.shape[0]
    F = w_ffn_out.shape[1]
    Fp = -(-F // 1024) * 1024
    ret_qkw, ret_vw = RET_H * RET_DK, RET_H * RET_DV
    gla_kw, gla_vw = GLA_H * GLA_DK, GLA_H * GLA_DV
    even_main = 2 * ret_qkw + 2 * ret_vw + 2 * gla_kw + 2 * gla_vw

    cond = jnp.zeros((MOD_ROWS, D), F32).at[0].set(c_ctx).at[1:1 + Bs].set(c)
    mods = mod_matmul(cond, w_mod, b_mod).reshape(depth, MOD_ROWS, 1, N_MOD * D)

    ret_rope = _rope_tables(Ts, RET_DK)
    da_cos, da_sin = _rope_tables(Ts, DA_DH)
    da_rope = (jnp.concatenate([da_cos, da_cos], axis=-1), jnp.concatenate([-da_sin, da_sin], axis=-1))

    streams = [dict(T=Tp, g0=0, rpg=Bp * Tp), dict(T=Ts, g0=1, rpg=Ts)]
    ys = [x_prompt.reshape(Bp * Tp, D), x_sample.reshape(Bs * Ts, D)]
    hs = [rowwise(y, mods, st["g0"], st["rpg"], mod=(norm_g[0, 0], 0, 0, 1))[0] for y, st in zip(ys, streams)]

    ret_states, gla_states, ks_out, vs_out = [], [], [], []
    for l in range(depth):
        if l % 2 == 0:
            e = l // 2
            w_main = w_even_in[e, :, :even_main].astype(BF16)
            w_rank = _pad_cols(w_even_in[e, :, even_main:], LANES).astype(BF16)
            w_out = w_even_out[e].astype(BF16)
            dec = jnp.broadcast_to(ret_decay[e].T[:, :, None, None], (RET_H, 2, 8, LANES))
            w2 = gla_w2[e].reshape(2, GLA_RANK, GLA_H, GLA_DK).transpose(2, 0, 1, 3)
            w2p = jnp.zeros((GLA_H, 2, LANES, GLA_DK), F32)
            w2p = w2p.at[:, 0, :GLA_RANK].set(w2[:, 0]).at[:, 1, GLA_RANK:2 * GLA_RANK].set(w2[:, 1]).astype(BF16)
            gbias = gla_b[e].reshape(2, GLA_H, 1, GLA_DK).transpose(1, 0, 2, 3)
            mix = []
            for si, st in enumerate(streams):
                z = matmul(hs[si], w_main, BF16)
                zr = matmul(hs[si], w_rank, BF16)
                lat = si == 1
                s0r = (state_ret, e) if lat else None
                s0g = (state_gla, e) if lat else None
                ro = scan_mixer(z, None, T=st["T"], col_q=0, col_k=ret_qkw, col_v=2 * ret_qkw,
                                col_gate=2 * ret_qkw + ret_vw, H=RET_H, K=RET_DK, V=RET_DV,
                                norm_g=ret_norm_g[e], is_gla=False, dec=dec,
                                rope=ret_rope if lat else None, s0=s0r, emit_state=not lat)
                g0 = 2 * ret_qkw + 2 * ret_vw
                go = scan_mixer(z, zr, T=st["T"], col_q=g0, col_k=g0 + gla_kw, col_v=g0 + 2 * gla_kw,
                                col_gate=g0 + 2 * gla_kw + gla_vw, H=GLA_H, K=GLA_DK, V=GLA_DV,
                                norm_g=gla_norm_g[e], is_gla=True, w2=w2p, gbias=gbias,
                                s0=s0g, emit_state=not lat)
                if not lat:
                    ret_states.append(ro[1])
                    gla_states.append(go[1])
                mix.append(matmul(jnp.concatenate([ro[0], go[0]], axis=-1), w_out, F32))
        else:
            o = l // 2
            lam_init = 0.8 - 0.6 * math.exp(-0.3 * l)
            w_in = w_odd_in[o].astype(BF16)
            w_out = w_odd_out[o].astype(BF16)
            ctx = (cache_k[:, o].reshape(Bs, -1, DA_H * 2 * DA_DH).astype(BF16),
                   cache_v[:, o].reshape(Bs, -1, DA_H * 2 * DA_DH).astype(BF16))
            mix = []
            for si, st in enumerate(streams):
                z = matmul(hs[si], w_in, BF16)
                lat = si == 1
                a = diff_attention(z, T=st["T"], lam_p=da_lambda[o], subln_g=da_subln_g[o], lam_init=lam_init,
                                   ctx=ctx if lat else None, rope=da_rope if lat else None)
                if not lat:
                    kv_w = DA_H * 2 * DA_DH
                    ks_out.append(z[:, kv_w:2 * kv_w].astype(F32).reshape(Bp, Tp, DA_H, 2 * DA_DH))
                    vs_out.append(z[:, 2 * kv_w:].astype(F32).reshape(Bp, Tp, DA_H, 2 * DA_DH))
                mix.append(matmul(a, w_out, F32))

        wf = w_ffn_in[l].reshape(D, 2, F)
        wg = _pad_cols(wf[:, 0], Fp).astype(BF16)
        wv = _pad_cols(wf[:, 1], Fp).astype(BF16)
        cw = _pad_cols(ffn_conv_w[l].reshape(3, 2, F).transpose(1, 0, 2), Fp)
        cb = _pad_cols(ffn_conv_b[l].reshape(2, 1, F), Fp)
        wo = jnp.pad(w_ffn_out[l], ((0, Fp - F), (0, 0))).astype(BF16)
        for si, st in enumerate(streams):
            ys[si], h2 = rowwise(ys[si], mods, st["g0"], st["rpg"],
                                 resid=(mix[si], norm_g[l, 1], l, 2), mod=(norm_g[l, 2], l, 3, 4))
            act = ffn_in(h2, wg, wv, cw, cb, st["T"])
            f = matmul(act, wo, F32, tk=Fp // 4)
            nxt = (norm_g[l + 1, 0], l + 1, 0, 1) if l + 1 < depth else None
            res = rowwise(ys[si], mods, st["g0"], st["rpg"], resid=(f, norm_g[l, 3], l, 5), mod=nxt)
            ys[si] = res[0]
            hs[si] = res[1] if nxt is not None else None

    new_state_ret = jnp.stack(ret_states, axis=1)
    new_state_gla = jnp.stack(gla_states, axis=1)
    new_cache_k = jnp.stack(ks_out, axis=1)
    new_cache_v = jnp.stack(vs_out, axis=1)
    return (ys[0].reshape(Bp, Tp, D), ys[1].reshape(Bs, Ts, D),
            new_state_ret, new_state_gla, new_cache_k, new_cache_v)
```

```python
import functools
import math

import jax
import jax.numpy as jnp
from jax import lax
from jax.experimental import pallas as pl
from jax.experimental.pallas import tpu as pltpu

F32 = jnp.float32
BF16 = jnp.bfloat16

V7X_VMEM_LIMIT_BYTES = 58 * 1024 * 1024
LANES = 128
HALO = 8

GRID_W = 64
RET_H, RET_DK, RET_DV = 8, 256, 256
GLA_H, GLA_DK, GLA_DV = 4, 256, 512
GLA_RANK = 16
GLA_TAU = 16.0
DA_H, DA_DH = 16, 128
CHUNK = 64
ROPE_BASE = 10000.0
EPS = 1e-6
N_MOD = 6
MOD_ROWS = 16


def _cparams(sem):
    return pltpu.CompilerParams(dimension_semantics=sem, vmem_limit_bytes=V7X_VMEM_LIMIT_BYTES)


def _dot(a, b):
    return jnp.dot(a, b, preferred_element_type=F32)


def _dot_nt(a, b):
    return lax.dot_general(a, b, (((1,), (1,)), ((), ())), preferred_element_type=F32)


def _dot_tn(a, b):
    return lax.dot_general(a, b, (((0,), (0,)), ((), ())), preferred_element_type=F32)


def _silu(x):
    return x * (1.0 / (1.0 + jnp.exp(-x)))


def _mod_kernel(c_ref, w_ref, b_ref, o_ref):
    s = _silu(c_ref[...]).astype(BF16)
    o_ref[...] = _dot(s, w_ref[...].astype(BF16)) + b_ref[...]


def mod_matmul(cond, w_mod, b_mod, tn=512):
    L, D, N = w_mod.shape
    return pl.pallas_call(
        _mod_kernel,
        grid=(L, N // tn),
        in_specs=[
            pl.BlockSpec((MOD_ROWS, D), lambda l, j: (0, 0)),
            pl.BlockSpec((None, D, tn), lambda l, j: (l, 0, j)),
            pl.BlockSpec((None, 1, tn), lambda l, j: (l, 0, j)),
        ],
        out_specs=pl.BlockSpec((None, MOD_ROWS, tn), lambda l, j: (l, 0, j)),
        out_shape=jax.ShapeDtypeStruct((L, MOD_ROWS, N), F32),
        compiler_params=_cparams(("arbitrary", "arbitrary")),
        name="mod_matmul",
    )(cond, w_mod, b_mod.reshape(L, 1, N))


def _rms(x):
    return x * lax.rsqrt(jnp.mean(x * x, axis=-1, keepdims=True) + EPS)


def _rowwise_kernel(*refs, has_resid, has_mod):
    refs = list(refs)
    y_ref = refs.pop(0)
    y = y_ref[...]
    if has_resid:
        o_ref, gpost_ref, gate_ref = refs.pop(0), refs.pop(0), refs.pop(0)
        y = y + gate_ref[...] * (_rms(o_ref[...].astype(F32)) * gpost_ref[...])
    if has_mod:
        gpre_ref, shift_ref, scale_ref = refs.pop(0), refs.pop(0), refs.pop(0)
    if has_resid:
        ynew_ref = refs.pop(0)
        ynew_ref[...] = y
    if has_mod:
        h_ref = refs.pop(0)
        h = (_rms(y) * gpre_ref[...]) * (1.0 + scale_ref[...]) + shift_ref[...]
        h_ref[...] = h.astype(BF16)


def rowwise(y, mods, group0, rows_per_group, *, resid=None, mod=None, tr=256):
    M, D = y.shape
    tr = min(tr, M)

    def mod_spec(layer, chunk):
        return pl.BlockSpec((None, None, 1, D),
                            lambda i: (layer, group0 + (i * tr) // rows_per_group, 0, chunk))

    row_spec = pl.BlockSpec((tr, D), lambda i: (i, 0))
    vec_spec = pl.BlockSpec((1, D), lambda i: (0, 0))
    args, in_specs = [y], [row_spec]
    out_shape, out_specs = [], []
    if resid is not None:
        o, g_post, layer, gate_chunk = resid
        args += [o, g_post.reshape(1, D), mods]
        in_specs += [row_spec, vec_spec, mod_spec(layer, gate_chunk)]
        out_shape.append(jax.ShapeDtypeStruct((M, D), F32))
        out_specs.append(row_spec)
    if mod is not None:
        g_pre, layer, shift_chunk, scale_chunk = mod
        args += [g_pre.reshape(1, D), mods, mods]
        in_specs += [vec_spec, mod_spec(layer, shift_chunk), mod_spec(layer, scale_chunk)]
        out_shape.append(jax.ShapeDtypeStruct((M, D), BF16))
        out_specs.append(row_spec)
    return pl.pallas_call(
        functools.partial(_rowwise_kernel, has_resid=resid is not None, has_mod=mod is not None),
        grid=(M // tr,),
        in_specs=in_specs,
        out_specs=out_specs,
        out_shape=out_shape,
        compiler_params=_cparams(("arbitrary",)),
        name="rowwise",
    )(*args)


def _mm_kernel(x_ref, w_ref, o_ref):
    o_ref[...] = _dot(x_ref[...], w_ref[...]).astype(o_ref.dtype)


def _mm_acc_kernel(x_ref, w_ref, o_ref, acc_ref):
    k = pl.program_id(2)

    @pl.when(k == 0)
    def _():
        acc_ref[...] = jnp.zeros_like(acc_ref)

    acc_ref[...] += _dot(x_ref[...], w_ref[...])

    @pl.when(k == pl.num_programs(2) - 1)
    def _():
        o_ref[...] = acc_ref[...].astype(o_ref.dtype)


def matmul(x, w, out_dtype, tm=1024, tn=1024, tk=None):
    M, K = x.shape
    N = w.shape[1]
    tm, tn = min(tm, M), min(tn, N)
    if tk is None or tk >= K:
        return pl.pallas_call(
            _mm_kernel,
            grid=(M // tm, N // tn),
            in_specs=[pl.BlockSpec((tm, K), lambda i, j: (i, 0)),
                      pl.BlockSpec((K, tn), lambda i, j: (0, j))],
            out_specs=pl.BlockSpec((tm, tn), lambda i, j: (i, j)),
            out_shape=jax.ShapeDtypeStruct((M, N), out_dtype),
            compiler_params=_cparams(("arbitrary", "arbitrary")),
            name="matmul",
        )(x, w)
    return pl.pallas_call(
        _mm_acc_kernel,
        grid=(M // tm, N // tn, K // tk),
        in_specs=[pl.BlockSpec((tm, tk), lambda i, j, k: (i, k)),
                  pl.BlockSpec((tk, tn), lambda i, j, k: (k, j))],
        out_specs=pl.BlockSpec((tm, tn), lambda i, j, k: (i, j)),
        out_shape=jax.ShapeDtypeStruct((M, N), out_dtype),
        scratch_shapes=[pltpu.VMEM((tm, tn), F32)],
        compiler_params=_cparams(("arbitrary", "arbitrary", "arbitrary")),
        name="matmul_kgrid",
    )(x, w)


def _ffn_in_kernel(h_ref, hprev_ref, hnext_ref, wg_ref, wv_ref, cw_ref, cb_ref, o_ref, hext_ref,
                   *, tm, tfs, seq_len):
    i = pl.program_id(0)

    @pl.when(pl.program_id(1) == 0)
    def _():
        hext_ref[0:HALO, :] = hprev_ref[...]
        hext_ref[HALO:HALO + tm, :] = h_ref[...]
        hext_ref[HALO + tm:, :] = hnext_ref[...]

    hext = hext_ref[...]
    rows = tm + 2 * HALO
    pos = (i * tm + lax.broadcasted_iota(jnp.int32, (tm, 1), 0)) % seq_len
    not_first = (pos != 0).astype(F32)
    not_last = (pos != seq_len - 1).astype(F32)

    def conv(w_ref, half, cols):
        u = _dot(hext, w_ref[:, cols])
        u_prev = pltpu.roll(u, 1, axis=0)[HALO:HALO + tm]
        u_next = pltpu.roll(u, rows - 1, axis=0)[HALO:HALO + tm]
        cw = cw_ref[half][:, cols]
        return (u_prev * not_first * cw[0:1] + u[HALO:HALO + tm] * cw[1:2]
                + u_next * not_last * cw[2:3] + cb_ref[half][:, cols])

    for c0 in range(0, o_ref.shape[1], tfs):
        cols = slice(c0, c0 + tfs)
        g = conv(wg_ref, 0, cols)
        val = conv(wv_ref, 1, cols)
        o_ref[:, cols] = (_silu(g) * val).astype(o_ref.dtype)


def ffn_in(h, wg, wv, cw, cb, seq_len, tm=1024, tf=512, tfs=256):
    M, D = h.shape
    Fp = wg.shape[1]
    tm, tf = min(tm, M), min(tf, Fp)
    tfs = min(tfs, tf)
    nb = tm // HALO
    last = M // HALO - 1
    return pl.pallas_call(
        functools.partial(_ffn_in_kernel, tm=tm, tfs=tfs, seq_len=seq_len),
        grid=(M // tm, Fp // tf),
        in_specs=[
            pl.BlockSpec((tm, D), lambda i, j: (i, 0)),
            pl.BlockSpec((HALO, D), lambda i, j: (jnp.maximum(i * nb - 1, 0), 0)),
            pl.BlockSpec((HALO, D), lambda i, j: (jnp.minimum((i + 1) * nb, last), 0)),
            pl.BlockSpec((D, tf), lambda i, j: (0, j)),
            pl.BlockSpec((D, tf), lambda i, j: (0, j)),
            pl.BlockSpec((2, 3, tf), lambda i, j: (0, 0, j)),
            pl.BlockSpec((2, 1, tf), lambda i, j: (0, 0, j)),
        ],
        out_specs=pl.BlockSpec((tm, tf), lambda i, j: (i, j)),
        out_shape=jax.ShapeDtypeStruct((M, Fp), BF16),
        scratch_shapes=[pltpu.VMEM((tm + 2 * HALO, D), BF16)],
        compiler_params=_cparams(("arbitrary", "arbitrary")),
        name="ffn_in",
    )(h, h, h, wg, wv, cw, cb)


def _scan_kernel(*refs, T, K, V, is_gla, use_rope, has_s0, emit_state, q_scale, k_scale):
    refs = list(refs)
    q_ref, k_ref, v_ref, gate_ref, ng_ref = (refs.pop(0) for _ in range(5))
    if is_gla:
        r_ref, w2_ref, gb_ref = (refs.pop(0) for _ in range(3))
    else:
        dec_ref = refs.pop(0)
    if use_rope:
        cos_ref, sin_ref = refs.pop(0), refs.pop(0)
    if has_s0:
        s0_ref = refs.pop(0)
    out_ref = refs.pop(0)
    if emit_state:
        sfin_ref = refs.pop(0)
    s_ref, osum_ref = refs.pop(0), refs.pop(0)

    C = CHUNK
    n = T // C
    half = K // 2
    ti = lax.broadcasted_iota(jnp.int32, (C, C), 0)
    si = lax.broadcasted_iota(jnp.int32, (C, C), 1)
    tri = (ti >= si, si >= ti)
    tcol = lax.broadcasted_iota(jnp.int32, (C, 1), 0).astype(F32)

    for d in range(2):
        if has_s0:
            s_ref[d] = s0_ref[d]
        else:
            s_ref[d] = jnp.zeros((K, V), F32)

    if is_gla:
        cum = tuple(m.astype(F32) for m in tri)
    else:
        lg = tuple(-jnp.exp(dec_ref[d][0:1, 0:1]) for d in range(2))
        dmat = (jnp.where(tri[0], jnp.exp((ti - si).astype(F32) * lg[0]), 0.0),
                jnp.where(tri[1], jnp.exp((si - ti).astype(F32) * lg[1]), 0.0))
        e_in = (jnp.exp((tcol + 1.0) * lg[0]), jnp.exp((C - tcol) * lg[1]))
        e_out = (jnp.exp((C - 1.0 - tcol) * lg[0]), jnp.exp(tcol * lg[1]))
        e_tot = (jnp.exp(C * lg[0]), jnp.exp(C * lg[1]))

    def load_qk(ref, rows, scale):
        x = ref[rows, :].astype(F32)
        if use_rope:
            cs, sn = cos_ref[rows, :], sin_ref[rows, :]
            x1, x2 = x[:, :half], x[:, half:]
            x = jnp.concatenate([x1 * cs - x2 * sn, x1 * sn + x2 * cs], axis=-1)
        if scale != 1.0:
            x = x * scale
        return x

    def chunk_step(c, d):
        cc = c if d == 0 else n - 1 - c
        rows = pl.ds(pl.multiple_of(cc * C, C), C)
        qc = load_qk(q_ref, rows, q_scale)
        kc = load_qk(k_ref, rows, k_scale)
        vc = v_ref[rows, :]
        s_old = s_ref[d]
        if is_gla:
            logits = _dot(r_ref[rows, :], w2_ref[d]) + gb_ref[d]
            la = (jnp.minimum(logits, 0.0) - jnp.log(1.0 + jnp.exp(-jnp.abs(logits)))) * (1.0 / GLA_TAU)
            b = jnp.dot(cum[d], la, preferred_element_type=F32, precision=lax.Precision.HIGHEST)
            b_end = b[C - 1:C, :] if d == 0 else b[0:1, :]
            q_in = (qc * jnp.exp(b)).astype(BF16)
            k_in = (kc * jnp.exp(-b)).astype(BF16)
            k_out = kc * jnp.exp(b_end - b)
            att = jnp.where(tri[d], _dot_nt(q_in, k_in), 0.0)
            s_decay = jnp.exp(jnp.sum(la.T, axis=1, keepdims=True))
        else:
            att = _dot_nt(qc.astype(BF16), kc.astype(BF16)) * dmat[d]
            q_in = (qc * e_in[d]).astype(BF16)
            k_out = kc * e_out[d]
            s_decay = e_tot[d]
        o = _dot(att.astype(BF16), vc) + _dot(q_in, s_old.astype(BF16))
        s_ref[d] = s_decay * s_old + _dot(k_out.T.astype(BF16), vc)
        return rows, o

    def first_half(c, carry):
        for d in range(2):
            rows, o = chunk_step(c, d)
            osum_ref[rows, :] = o
        return carry

    def second_half(c, carry):
        for d in range(2):
            rows, o = chunk_step(c, d)
            o = o + osum_ref[rows, :]
            gate = gate_ref[rows, :].astype(F32)
            out_ref[rows, :] = (_rms(o) * ng_ref[...] * _silu(gate)).astype(out_ref.dtype)
        return carry

    unroll = min(n // 2, 2 if is_gla else 4)
    lax.fori_loop(0, n // 2, first_half, 0, unroll=unroll)
    lax.fori_loop(n // 2, n, second_half, 0, unroll=unroll)
    if emit_state:
        for d in range(2):
            sfin_ref[d] = s_ref[d]


def scan_mixer(z, zr, *, T, col_q, col_k, col_v, col_gate, H, K, V, norm_g, is_gla,
               dec=None, w2=None, gbias=None, rope=None, s0=None, emit_state=False):
    M = z.shape[0]
    B = M // T
    args = [z, z, z, z, norm_g.reshape(1, H * V)]
    in_specs = [
        pl.BlockSpec((T, K), lambda b, h: (b, col_q // K + h)),
        pl.BlockSpec((T, K), lambda b, h: (b, col_k // K + h)),
        pl.BlockSpec((T, V), lambda b, h: (b, col_v // V + h)),
        pl.BlockSpec((T, V), lambda b, h: (b, col_gate // V + h)),
        pl.BlockSpec((1, V), lambda b, h: (0, h)),
    ]
    if is_gla:
        args += [zr, w2, gbias]
        in_specs += [pl.BlockSpec((T, LANES), lambda b, h: (b, 0)),
                     pl.BlockSpec((None, 2, LANES, K), lambda b, h: (h, 0, 0, 0)),
                     pl.BlockSpec((None, 2, 1, K), lambda b, h: (h, 0, 0, 0))]
    else:
        args += [dec]
        in_specs += [pl.BlockSpec((None, 2, 8, LANES), lambda b, h: (h, 0, 0, 0))]
    if rope is not None:
        args += [rope[0], rope[1]]
        in_specs += [pl.BlockSpec((T, K // 2), lambda b, h: (0, 0))] * 2
    if s0 is not None:
        s0_arr, s0_e = s0
        args += [s0_arr]
        in_specs += [pl.BlockSpec((None, None, 2, None, K, V), lambda b, h: (b, s0_e, 0, h, 0, 0))]
    out_shape = [jax.ShapeDtypeStruct((M, H * V), BF16)]
    out_specs = [pl.BlockSpec((T, V), lambda b, h: (b, h))]
    if emit_state:
        out_shape.append(jax.ShapeDtypeStruct((B, 2, H, K, V), F32))
        out_specs.append(pl.BlockSpec((None, 2, None, K, V), lambda b, h: (b, 0, h, 0, 0)))
    kern = functools.partial(
        _scan_kernel, T=T, K=K, V=V, is_gla=is_gla, use_rope=rope is not None, has_s0=s0 is not None,
        emit_state=emit_state, q_scale=(K ** -0.5 if is_gla else 1.0), k_scale=(1.0 if is_gla else K ** -0.5))
    return pl.pallas_call(
        kern,
        grid=(B, H),
        in_specs=in_specs,
        out_specs=out_specs,
        out_shape=out_shape,
        scratch_shapes=[pltpu.VMEM((2, K, V), F32), pltpu.VMEM((T, V), F32)],
        compiler_params=_cparams(("arbitrary", "arbitrary")),
        name="scan_gla" if is_gla else "scan_ret",
    )(*args)


def _attn_kernel(*refs, T, tq, tk, L, use_rope, lam_init):
    refs = list(refs)
    q_ref, k_ref, v_ref, lam_ref, g_ref = (refs.pop(0) for _ in range(5))
    if L:
        ck_ref, cv_ref = refs.pop(0), refs.pop(0)
    if use_rope:
        cos_ref, sin_ref = refs.pop(0), refs.pop(0)
    o_ref = refs.pop(0)
    if use_rope:
        kr_ref = refs.pop(0)
    Dh = DA_DH
    qi = pl.program_id(2)
    scale = Dh ** -0.5 * math.log2(math.e)

    def rope(x, rows):
        cs, sn = cos_ref[rows, :], sin_ref[rows, :]
        return jnp.concatenate(
            [x[:, m * Dh:(m + 1) * Dh] * cs + pltpu.roll(x[:, m * Dh:(m + 1) * Dh], Dh // 2, axis=1) * sn
             for m in range(2)], axis=-1)

    if use_rope:
        @pl.when(qi == 0)
        def _():
            def body(j, carry):
                rows = pl.ds(pl.multiple_of(j * tk, tk), tk)
                kr_ref[rows, :] = rope(k_ref[rows, :].astype(F32), rows).astype(BF16)
                return carry
            lax.fori_loop(0, T // tk, body, 0)

    qrows = pl.ds(pl.multiple_of(qi * tq, tq), tq)
    q = q_ref[...].astype(F32)
    if use_rope:
        q = rope(q, qrows)
    q = (q * scale).astype(BF16)
    qs = (q[:, :Dh], q[:, Dh:])

    def lane_fold(x, op):
        parts = [x[:, c:c + LANES] for c in range(0, x.shape[1], LANES)]
        while len(parts) > 1:
            parts = [op(parts[i], parts[i + 1]) for i in range(0, len(parts) - 1, 2)] + (
                [parts[-1]] if len(parts) % 2 else [])
        return parts[0]

    def update(carry, kblk, vblk):
        new = []
        for m in range(2):
            mx, l, acc = carry[m]
            s = _dot_nt(qs[m], kblk[:, m * Dh:(m + 1) * Dh])
            mx_new = jnp.maximum(mx, jnp.max(lane_fold(s, jnp.maximum), axis=-1, keepdims=True))
            alpha = jnp.exp2(mx - mx_new)
            p = jnp.exp2(s - mx_new)
            l = alpha * l + jnp.sum(lane_fold(p, jnp.add), axis=-1, keepdims=True)
            acc = alpha * acc + _dot(p.astype(BF16), vblk)
            new.append((mx_new, l, acc))
        return tuple(new)

    carry = tuple((jnp.full((tq, 1), -jnp.inf, F32), jnp.zeros((tq, 1), F32), jnp.zeros((tq, 2 * Dh), F32))
                  for _ in range(2))
    ksrc = kr_ref if use_rope else k_ref
    for j in range(T // tk):
        carry = update(carry, ksrc[j * tk:(j + 1) * tk, :], v_ref[j * tk:(j + 1) * tk, :])
    if L:
        carry = update(carry, ck_ref[...], cv_ref[...])

    lp = lam_ref[...]
    lam = (jnp.exp(jnp.sum(lp[0:1] * lp[1:2], axis=-1, keepdims=True))
           - jnp.exp(jnp.sum(lp[2:3] * lp[3:4], axis=-1, keepdims=True)) + lam_init)
    (_, l1, a1), (_, l2, a2) = carry
    o = a1 / l1 - lam * (a2 / l2)
    o_ref[...] = (_rms(o) * g_ref[...] * (1.0 - lam_init)).astype(o_ref.dtype)


def diff_attention(z, *, T, lam_p, subln_g, lam_init, ctx=None, rope=None, tq=512, tk=1024):
    M = z.shape[0]
    B = M // T
    H, W = DA_H, 2 * DA_DH
    tq, tk = min(tq, T), min(tk, T)
    nq = T // tq
    args = [z, z, z, lam_p, subln_g.reshape(1, W)]
    in_specs = [
        pl.BlockSpec((tq, W), lambda b, h, i: (b * nq + i, h)),
        pl.BlockSpec((T, W), lambda b, h, i: (b, H + h)),
        pl.BlockSpec((T, W), lambda b, h, i: (b, 2 * H + h)),
        pl.BlockSpec((4, DA_DH), lambda b, h, i: (0, 0)),
        pl.BlockSpec((1, W), lambda b, h, i: (0, 0)),
    ]
    L = 0
    if ctx is not None:
        L = ctx[0].shape[1]
        args += [ctx[0], ctx[1]]
        in_specs += [pl.BlockSpec((None, L, W), lambda b, h, i: (b, 0, h))] * 2
    scratch = []
    if rope is not None:
        args += [rope[0], rope[1]]
        in_specs += [pl.BlockSpec((T, DA_DH), lambda b, h, i: (0, 0))] * 2
        scratch.append(pltpu.VMEM((T, W), BF16))
    return pl.pallas_call(
        functools.partial(_attn_kernel, T=T, tq=tq, tk=tk, L=L, use_rope=rope is not None, lam_init=lam_init),
        grid=(B, H, nq),
        in_specs=in_specs,
        out_specs=pl.BlockSpec((tq, W), lambda b, h, i: (b * nq + i, h)),
        out_shape=jax.ShapeDtypeStruct((M, H * W), BF16),
        scratch_shapes=scratch,
        compiler_params=_cparams(("arbitrary", "arbitrary", "arbitrary")),
        name="diff_attention",
    )(*args)


def _rope_tables(n_tokens, head_dim):
    n_rows = n_tokens // GRID_W
    row = jnp.repeat(jnp.arange(n_rows), GRID_W).astype(F32)
    col = jnp.tile(jnp.arange(GRID_W), n_rows).astype(F32)
    quarter = head_dim // 4
    inv = ROPE_BASE ** (-jnp.arange(quarter, dtype=F32) / quarter)
    ang = jnp.concatenate([row[:, None] * inv, col[:, None] * inv], axis=-1)
    return jnp.cos(ang), jnp.sin(ang)


def _pad_cols(a, n):
    return jnp.pad(a, [(0, 0)] * (a.ndim - 1) + [(0, n - a.shape[-1])])


def kernel(x_prompt, x_sample, state_ret, state_gla, cache_k, cache_v, c, c_ctx, w_mod, b_mod, norm_g,
           w_even_in, w_even_out, ret_decay, ret_norm_g, gla_w2, gla_b, gla_norm_g,
           w_odd_in, w_odd_out, da_lambda, da_subln_g, w_ffn_in, ffn_conv_w, ffn_conv_b, w_ffn_out):
    Bp, Tp, D = x_prompt.shape
    Bs, Ts, _ = x_sample.shape
    depth = w_mod.shape[0]
    F = w_ffn_out.shape[1]
    Fp = -(-F // 1024) * 1024
    ret_qkw, ret_vw = RET_H * RET_DK, RET_H * RET_DV
    gla_kw, gla_vw = GLA_H * GLA_DK, GLA_H * GLA_DV
    even_main = 2 * ret_qkw + 2 * ret_vw + 2 * gla_kw + 2 * gla_vw

    cond = jnp.zeros((MOD_ROWS, D), F32).at[0].set(c_ctx).at[1:1 + Bs].set(c)
    mods = mod_matmul(cond, w_mod, b_mod).reshape(depth, MOD_ROWS, 1, N_MOD * D)

    ret_rope = _rope_tables(Ts, RET_DK)
    da_cos, da_sin = _rope_tables(Ts, DA_DH)
    da_rope = (jnp.concatenate([da_cos, da_cos], axis=-1), jnp.concatenate([-da_sin, da_sin], axis=-1))

    streams = [dict(T=Tp, g0=0, rpg=Bp * Tp), dict(T=Ts, g0=1, rpg=Ts)]
    ys = [x_prompt.reshape(Bp * Tp, D), x_sample.reshape(Bs * Ts, D)]
    hs = [rowwise(y, mods, st["g0"], st["rpg"], mod=(norm_g[0, 0], 0, 0, 1))[0] for y, st in zip(ys, streams)]

    ret_states, gla_states, ks_out, vs_out = [], [], [], []
    for l in range(depth):
        if l % 2 == 0:
            e = l // 2
            w_main = w_even_in[e, :, :even_main].astype(BF16)
            w_rank = _pad_cols(w_even_in[e, :, even_main:], LANES).astype(BF16)
            w_out = w_even_out[e].astype(BF16)
            dec = jnp.broadcast_to(ret_decay[e].T[:, :, None, None], (RET_H, 2, 8, LANES))
            w2 = gla_w2[e].reshape(2, GLA_RANK, GLA_H, GLA_DK).transpose(2, 0, 1, 3)
            w2p = jnp.zeros((GLA_H, 2, LANES, GLA_DK), F32)
            w2p = w2p.at[:, 0, :GLA_RANK].set(w2[:, 0]).at[:, 1, GLA_RANK:2 * GLA_RANK].set(w2[:, 1]).astype(BF16)
            gbias = gla_b[e].reshape(2, GLA_H, 1, GLA_DK).transpose(1, 0, 2, 3)
            mix = []
            for si, st in enumerate(streams):
                z = matmul(hs[si], w_main, BF16)
                zr = matmul(hs[si], w_rank, BF16)
                lat = si == 1
                s0r = (state_ret, e) if lat else None
                s0g = (state_gla, e) if lat else None
                ro = scan_mixer(z, None, T=st["T"], col_q=0, col_k=ret_qkw, col_v=2 * ret_qkw,
                                col_gate=2 * ret_qkw + ret_vw, H=RET_H, K=RET_DK, V=RET_DV,
                                norm_g=ret_norm_g[e], is_gla=False, dec=dec,
                                rope=ret_rope if lat else None, s0=s0r, emit_state=not lat)
                g0 = 2 * ret_qkw + 2 * ret_vw
                go = scan_mixer(z, zr, T=st["T"], col_q=g0, col_k=g0 + gla_kw, col_v=g0 + 2 * gla_kw,
                                col_gate=g0 + 2 * gla_kw + gla_vw, H=GLA_H, K=GLA_DK, V=GLA_DV,
                                norm_g=gla_norm_g[e], is_gla=True, w2=w2p, gbias=gbias,
                                s0=s0g, emit_state=not lat)
                if not lat:
                    ret_states.append(ro[1])
                    gla_states.append(go[1])
                mix.append(matmul(jnp.concatenate([ro[0], go[0]], axis=-1), w_out, BF16))
        else:
            o = l // 2
            lam_init = 0.8 - 0.6 * math.exp(-0.3 * l)
            w_in = w_odd_in[o].astype(BF16)
            w_out = w_odd_out[o].astype(BF16)
            ctx = (cache_k[:, o].reshape(Bs, -1, DA_H * 2 * DA_DH).astype(BF16),
                   cache_v[:, o].reshape(Bs, -1, DA_H * 2 * DA_DH).astype(BF16))
            mix = []
            for si, st in enumerate(streams):
                z = matmul(hs[si], w_in, BF16)
                lat = si == 1
                a = diff_attention(z, T=st["T"], lam_p=da_lambda[o], subln_g=da_subln_g[o], lam_init=lam_init,
                                   ctx=ctx if lat else None, rope=da_rope if lat else None)
                if not lat:
                    kv_w = DA_H * 2 * DA_DH
                    ks_out.append(z[:, kv_w:2 * kv_w].astype(F32).reshape(Bp, Tp, DA_H, 2 * DA_DH))
                    vs_out.append(z[:, 2 * kv_w:].astype(F32).reshape(Bp, Tp, DA_H, 2 * DA_DH))
                mix.append(matmul(a, w_out, BF16))

        wf = w_ffn_in[l].reshape(D, 2, F)
        wg = _pad_cols(wf[:, 0], Fp).astype(BF16)
        wv = _pad_cols(wf[:, 1], Fp).astype(BF16)
        cw = _pad_cols(ffn_conv_w[l].reshape(3, 2, F).transpose(1, 0, 2), Fp)
        cb = _pad_cols(ffn_conv_b[l].reshape(2, 1, F), Fp)
        wo = jnp.pad(w_ffn_out[l], ((0, Fp - F), (0, 0))).astype(BF16)
        for si, st in enumerate(streams):
            ys[si], h2 = rowwise(ys[si], mods, st["g0"], st["rpg"],
                                 resid=(mix[si], norm_g[l, 1], l, 2), mod=(norm_g[l, 2], l, 3, 4))
            act = ffn_in(h2, wg, wv, cw, cb, st["T"])
            f = matmul(act, wo, BF16, tk=Fp // 4)
            nxt = (norm_g[l + 1, 0], l + 1, 0, 1) if l + 1 < depth else None
            res = rowwise(ys[si], mods, st["g0"], st["rpg"], resid=(f, norm_g[l, 3], l, 5), mod=nxt)
            ys[si] = res[0]
            hs[si] = res[1] if nxt is not None else None

    new_state_ret = jnp.stack(ret_states, axis=1)
    new_state_gla = jnp.stack(gla_states, axis=1)
    new_cache_k = jnp.stack(ks_out, axis=1)
    new_cache_v = jnp.stack(vs_out, axis=1)
    return (ys[0].reshape(Bp, Tp, D), ys[1].reshape(Bs, Ts, D),
            new_state_ret, new_state_gla, new_cache_k, new_cache_v)
```

```python
import functools
import math

import jax
import jax.numpy as jnp
from jax import lax
from jax.experimental import pallas as pl
from jax.experimental.pallas import tpu as pltpu

F32 = jnp.float32
BF16 = jnp.bfloat16

V7X_VMEM_LIMIT_BYTES = 58 * 1024 * 1024
LANES = 128
HALO = 8

GRID_W = 64
RET_H, RET_DK, RET_DV = 8, 256, 256
GLA_H, GLA_DK, GLA_DV = 4, 256, 512
GLA_RANK = 16
GLA_TAU = 16.0
DA_H, DA_DH = 16, 128
CHUNK = 64
ROPE_BASE = 10000.0
EPS = 1e-6
N_MOD = 6
MOD_ROWS = 16


def _cparams(sem):
    return pltpu.CompilerParams(dimension_semantics=sem, vmem_limit_bytes=V7X_VMEM_LIMIT_BYTES)


def _dot(a, b):
    return jnp.dot(a, b, preferred_element_type=F32)


def _dot_nt(a, b):
    return lax.dot_general(a, b, (((1,), (1,)), ((), ())), preferred_element_type=F32)


def _dot_tn(a, b):
    return lax.dot_general(a, b, (((0,), (0,)), ((), ())), preferred_element_type=F32)


def _silu(x):
    return x * (1.0 / (1.0 + jnp.exp(-x)))


def _mod_kernel(c_ref, w_ref, b_ref, o_ref):
    s = _silu(c_ref[...]).astype(BF16)
    o_ref[...] = _dot(s, w_ref[...].astype(BF16)) + b_ref[...]


def mod_matmul(cond, w_mod, b_mod, tn=1024):
    L, D, N = w_mod.shape
    assert N % tn == 0, (N, tn)
    return pl.pallas_call(
        _mod_kernel,
        grid=(L, N // tn),
        in_specs=[
            pl.BlockSpec((MOD_ROWS, D), lambda l, j: (0, 0)),
            pl.BlockSpec((None, D, tn), lambda l, j: (l, 0, j)),
            pl.BlockSpec((None, 1, tn), lambda l, j: (l, 0, j)),
        ],
        out_specs=pl.BlockSpec((None, MOD_ROWS, tn), lambda l, j: (l, 0, j)),
        out_shape=jax.ShapeDtypeStruct((L, MOD_ROWS, N), F32),
        compiler_params=_cparams(("arbitrary", "arbitrary")),
        name="mod_matmul",
    )(cond, w_mod, b_mod.reshape(L, 1, N))


def _rms(x):
    return x * lax.rsqrt(jnp.mean(x * x, axis=-1, keepdims=True) + EPS)


def _rowwise_kernel(*refs, has_resid, has_mod):
    refs = list(refs)
    y_ref = refs.pop(0)
    y = y_ref[...]
    if has_resid:
        o_ref, gpost_ref, gate_ref = refs.pop(0), refs.pop(0), refs.pop(0)
        y = y + gate_ref[...] * (_rms(o_ref[...].astype(F32)) * gpost_ref[...])
    if has_mod:
        gpre_ref, shift_ref, scale_ref = refs.pop(0), refs.pop(0), refs.pop(0)
    if has_resid:
        ynew_ref = refs.pop(0)
        ynew_ref[...] = y
    if has_mod:
        h_ref = refs.pop(0)
        h = (_rms(y) * gpre_ref[...]) * (1.0 + scale_ref[...]) + shift_ref[...]
        h_ref[...] = h.astype(BF16)


def rowwise(y, mods, group0, rows_per_group, *, resid=None, mod=None, tr=256):
    M, D = y.shape
    tr = min(tr, M)

    def mod_spec(layer, chunk):
        return pl.BlockSpec((None, None, 1, D),
                            lambda i: (layer, group0 + (i * tr) // rows_per_group, 0, chunk))

    row_spec = pl.BlockSpec((tr, D), lambda i: (i, 0))
    vec_spec = pl.BlockSpec((1, D), lambda i: (0, 0))
    args, in_specs = [y], [row_spec]
    out_shape, out_specs = [], []
    if resid is not None:
        o, g_post, layer, gate_chunk = resid
        args += [o, g_post.reshape(1, D), mods]
        in_specs += [row_spec, vec_spec, mod_spec(layer, gate_chunk)]
        out_shape.append(jax.ShapeDtypeStruct((M, D), F32))
        out_specs.append(row_spec)
    if mod is not None:
        g_pre, layer, shift_chunk, scale_chunk = mod
        args += [g_pre.reshape(1, D), mods, mods]
        in_specs += [vec_spec, mod_spec(layer, shift_chunk), mod_spec(layer, scale_chunk)]
        out_shape.append(jax.ShapeDtypeStruct((M, D), BF16))
        out_specs.append(row_spec)
    return pl.pallas_call(
        functools.partial(_rowwise_kernel, has_resid=resid is not None, has_mod=mod is not None),
        grid=(M // tr,),
        in_specs=in_specs,
        out_specs=out_specs,
        out_shape=out_shape,
        compiler_params=_cparams(("arbitrary",)),
        name="rowwise",
    )(*args)


def _mm_kernel(x_ref, w_ref, o_ref):
    o_ref[...] = _dot(x_ref[...], w_ref[...]).astype(o_ref.dtype)


def _mm_acc_kernel(x_ref, w_ref, o_ref, acc_ref):
    k = pl.program_id(2)

    @pl.when(k == 0)
    def _():
        acc_ref[...] = jnp.zeros_like(acc_ref)

    acc_ref[...] += _dot(x_ref[...], w_ref[...])

    @pl.when(k == pl.num_programs(2) - 1)
    def _():
        o_ref[...] = acc_ref[...].astype(o_ref.dtype)


def matmul(x, w, out_dtype, tm=1024, tn=1024, tk=None):
    M, K = x.shape
    N = w.shape[1]
    tm, tn = min(tm, M), min(tn, N)
    if tk is None or tk >= K:
        return pl.pallas_call(
            _mm_kernel,
            grid=(M // tm, N // tn),
            in_specs=[pl.BlockSpec((tm, K), lambda i, j: (i, 0)),
                      pl.BlockSpec((K, tn), lambda i, j: (0, j))],
            out_specs=pl.BlockSpec((tm, tn), lambda i, j: (i, j)),
            out_shape=jax.ShapeDtypeStruct((M, N), out_dtype),
            compiler_params=_cparams(("arbitrary", "arbitrary")),
            name="matmul",
        )(x, w)
    return pl.pallas_call(
        _mm_acc_kernel,
        grid=(M // tm, N // tn, K // tk),
        in_specs=[pl.BlockSpec((tm, tk), lambda i, j, k: (i, k)),
                  pl.BlockSpec((tk, tn), lambda i, j, k: (k, j))],
        out_specs=pl.BlockSpec((tm, tn), lambda i, j, k: (i, j)),
        out_shape=jax.ShapeDtypeStruct((M, N), out_dtype),
        scratch_shapes=[pltpu.VMEM((tm, tn), F32)],
        compiler_params=_cparams(("arbitrary", "arbitrary", "arbitrary")),
        name="matmul_kgrid",
    )(x, w)


def _mm_pair_kernel(xa_ref, xb_ref, wa_ref, wb_ref, o_ref):
    o_ref[...] = (_dot(xa_ref[...], wa_ref[...]) + _dot(xb_ref[...], wb_ref[...])).astype(o_ref.dtype)


def matmul_pair(xa, xb, w, out_dtype, tm=1024, tn=1024):
    M, Kh = xa.shape
    N = w.shape[1]
    tm, tn = min(tm, M), min(tn, N)
    return pl.pallas_call(
        _mm_pair_kernel,
        grid=(M // tm, N // tn),
        in_specs=[pl.BlockSpec((tm, Kh), lambda i, j: (i, 0)),
                  pl.BlockSpec((tm, Kh), lambda i, j: (i, 0)),
                  pl.BlockSpec((Kh, tn), lambda i, j: (0, j)),
                  pl.BlockSpec((Kh, tn), lambda i, j: (1, j))],
        out_specs=pl.BlockSpec((tm, tn), lambda i, j: (i, j)),
        out_shape=jax.ShapeDtypeStruct((M, N), out_dtype),
        compiler_params=_cparams(("arbitrary", "arbitrary")),
        name="matmul_pair",
    )(xa, xb, w, w)


def _ffn_in_kernel(h_ref, hprev_ref, hnext_ref, wg_ref, wv_ref, cw_ref, cb_ref, o_ref, hext_ref,
                   *, tm, tfs, seq_len):
    i = pl.program_id(0)

    @pl.when(pl.program_id(1) == 0)
    def _():
        hext_ref[0:HALO, :] = hprev_ref[...]
        hext_ref[HALO:HALO + tm, :] = h_ref[...]
        hext_ref[HALO + tm:, :] = hnext_ref[...]

    hext = hext_ref[...]
    rows = tm + 2 * HALO
    pos = (i * tm + lax.broadcasted_iota(jnp.int32, (tm, 1), 0)) % seq_len
    not_first = (pos != 0).astype(F32)
    not_last = (pos != seq_len - 1).astype(F32)

    def conv(w_ref, half, cols):
        u = _dot(hext, w_ref[:, cols])
        u_prev = pltpu.roll(u, 1, axis=0)[HALO:HALO + tm]
        u_next = pltpu.roll(u, rows - 1, axis=0)[HALO:HALO + tm]
        cw = cw_ref[half][:, cols]
        return (u_prev * not_first * cw[0:1] + u[HALO:HALO + tm] * cw[1:2]
                + u_next * not_last * cw[2:3] + cb_ref[half][:, cols])

    for c0 in range(0, o_ref.shape[1], tfs):
        cols = slice(c0, c0 + tfs)
        g = conv(wg_ref, 0, cols)
        val = conv(wv_ref, 1, cols)
        o_ref[:, cols] = (_silu(g) * val).astype(o_ref.dtype)


def ffn_in(h, wg, wv, cw, cb, seq_len, tm=1024, tf=512, tfs=512):
    M, D = h.shape
    Fp = wg.shape[1]
    tm, tf = min(tm, M), min(tf, Fp)
    tfs = min(tfs, tf)
    nb = tm // HALO
    last = M // HALO - 1
    return pl.pallas_call(
        functools.partial(_ffn_in_kernel, tm=tm, tfs=tfs, seq_len=seq_len),
        grid=(M // tm, Fp // tf),
        in_specs=[
            pl.BlockSpec((tm, D), lambda i, j: (i, 0)),
            pl.BlockSpec((HALO, D), lambda i, j: (jnp.maximum(i * nb - 1, 0), 0)),
            pl.BlockSpec((HALO, D), lambda i, j: (jnp.minimum((i + 1) * nb, last), 0)),
            pl.BlockSpec((D, tf), lambda i, j: (0, j)),
            pl.BlockSpec((D, tf), lambda i, j: (0, j)),
            pl.BlockSpec((2, 3, tf), lambda i, j: (0, 0, j)),
            pl.BlockSpec((2, 1, tf), lambda i, j: (0, 0, j)),
        ],
        out_specs=pl.BlockSpec((tm, tf), lambda i, j: (i, j)),
        out_shape=jax.ShapeDtypeStruct((M, Fp), BF16),
        scratch_shapes=[pltpu.VMEM((tm + 2 * HALO, D), BF16)],
        compiler_params=_cparams(("arbitrary", "arbitrary")),
        name="ffn_in",
    )(h, h, h, wg, wv, cw, cb)


def _scan_kernel(*refs, T, K, V, is_gla, use_rope, has_s0, emit_state, q_scale, k_scale):
    refs = list(refs)
    q_ref, k_ref, v_ref, gate_ref, ng_ref = (refs.pop(0) for _ in range(5))
    if is_gla:
        r_ref, w2_ref, gb_ref = (refs.pop(0) for _ in range(3))
    else:
        dec_ref = refs.pop(0)
    if use_rope:
        cos_ref, sin_ref = refs.pop(0), refs.pop(0)
    if has_s0:
        s0_ref = refs.pop(0)
    out_ref = refs.pop(0)
    if emit_state:
        sfin_ref = refs.pop(0)
    s_ref, osum_ref = refs.pop(0), refs.pop(0)

    C = CHUNK
    n = T // C
    half = K // 2
    ti = lax.broadcasted_iota(jnp.int32, (C, C), 0)
    si = lax.broadcasted_iota(jnp.int32, (C, C), 1)
    tri = (ti >= si, si >= ti)
    trow = lax.broadcasted_iota(jnp.int32, (C, 1), 0)
    tcol = trow.astype(F32)

    for d in range(2):
        if has_s0:
            s_ref[d] = s0_ref[d]
        else:
            s_ref[d] = jnp.zeros((K, V), F32)

    if not is_gla:
        lg = tuple(-jnp.exp(dec_ref[d][0:1, 0:1]) for d in range(2))
        dmat = (jnp.where(tri[0], jnp.exp((ti - si).astype(F32) * lg[0]), 0.0),
                jnp.where(tri[1], jnp.exp((si - ti).astype(F32) * lg[1]), 0.0))
        e_in = (jnp.exp((tcol + 1.0) * lg[0]), jnp.exp((C - tcol) * lg[1]))
        e_out = (jnp.exp((C - 1.0 - tcol) * lg[0]), jnp.exp(tcol * lg[1]))
        e_tot = (jnp.exp(C * lg[0]), jnp.exp(C * lg[1]))

    def load_qk(ref, rows, scale):
        x = ref[rows, :].astype(F32)
        if use_rope:
            cs, sn = cos_ref[rows, :], sin_ref[rows, :]
            x1, x2 = x[:, :half], x[:, half:]
            x = jnp.concatenate([x1 * cs - x2 * sn, x1 * sn + x2 * cs], axis=-1)
        if scale != 1.0:
            x = x * scale
        return x

    def chunk_step(c, d):
        cc = c if d == 0 else n - 1 - c
        rows = pl.ds(pl.multiple_of(cc * C, C), C)
        qc = load_qk(q_ref, rows, q_scale)
        kc = load_qk(k_ref, rows, k_scale)
        vc = v_ref[rows, :]
        s_old = s_ref[d]
        if is_gla:
            logits = _dot(r_ref[rows, :], w2_ref[d]) + gb_ref[d]
            la = (jnp.minimum(logits, 0.0) - jnp.log(1.0 + jnp.exp(-jnp.abs(logits)))) * (1.0 / GLA_TAU)
            b = la
            sh = 1
            while sh < C:
                if d == 0:
                    b = b + jnp.where(trow >= sh, pltpu.roll(b, sh, axis=0), 0.0)
                else:
                    b = b + jnp.where(trow < C - sh, pltpu.roll(b, C - sh, axis=0), 0.0)
                sh *= 2
            b_end = b[C - 1:C, :] if d == 0 else b[0:1, :]
            q_in = (qc * jnp.exp(b)).astype(BF16)
            k_in = (kc * jnp.exp(-b)).astype(BF16)
            k_out = kc * jnp.exp(b_end - b)
            att = jnp.where(tri[d], _dot_nt(q_in, k_in), 0.0)
            s_decay = jnp.exp(jnp.broadcast_to(b_end, (LANES, K)).T[:, 0:1])
        else:
            att = _dot_nt(qc.astype(BF16), kc.astype(BF16)) * dmat[d]
            q_in = (qc * e_in[d]).astype(BF16)
            k_out = kc * e_out[d]
            s_decay = e_tot[d]
        o = _dot(att.astype(BF16), vc) + _dot(q_in, s_old.astype(BF16))
        s_ref[d] = s_decay * s_old + _dot(k_out.T.astype(BF16), vc)
        return rows, o

    def first_half(c, carry):
        for d in range(2):
            rows, o = chunk_step(c, d)
            osum_ref[rows, :] = o
        return carry

    def second_half(c, carry):
        for d in range(2):
            rows, o = chunk_step(c, d)
            o = o + osum_ref[rows, :]
            gate = gate_ref[rows, :].astype(F32)
            out_ref[rows, :] = (_rms(o) * ng_ref[...] * _silu(gate)).astype(out_ref.dtype)
        return carry

    unroll = min(n // 2, 2 if is_gla else 4)
    lax.fori_loop(0, n // 2, first_half, 0, unroll=unroll)
    lax.fori_loop(n // 2, n, second_half, 0, unroll=unroll)
    if emit_state:
        for d in range(2):
            sfin_ref[d] = s_ref[d]


def scan_mixer(z, zr, *, T, col_q, col_k, col_v, col_gate, H, K, V, norm_g, is_gla,
               dec=None, w2=None, gbias=None, rope=None, s0=None, emit_state=False):
    M = z.shape[0]
    B = M // T
    args = [z, z, z, z, norm_g.reshape(1, H * V)]
    in_specs = [
        pl.BlockSpec((T, K), lambda b, h: (b, col_q // K + h)),
        pl.BlockSpec((T, K), lambda b, h: (b, col_k // K + h)),
        pl.BlockSpec((T, V), lambda b, h: (b, col_v // V + h)),
        pl.BlockSpec((T, V), lambda b, h: (b, col_gate // V + h)),
        pl.BlockSpec((1, V), lambda b, h: (0, h)),
    ]
    if is_gla:
        args += [zr, w2, gbias]
        in_specs += [pl.BlockSpec((T, LANES), lambda b, h: (b, 0)),
                     pl.BlockSpec((None, 2, LANES, K), lambda b, h: (h, 0, 0, 0)),
                     pl.BlockSpec((None, 2, 1, K), lambda b, h: (h, 0, 0, 0))]
    else:
        args += [dec]
        in_specs += [pl.BlockSpec((None, 2, 8, LANES), lambda b, h: (h, 0, 0, 0))]
    if rope is not None:
        args += [rope[0], rope[1]]
        in_specs += [pl.BlockSpec((T, K // 2), lambda b, h: (0, 0))] * 2
    if s0 is not None:
        s0_arr, s0_e = s0
        args += [s0_arr]
        in_specs += [pl.BlockSpec((None, None, 2, None, K, V), lambda b, h: (b, s0_e, 0, h, 0, 0))]
    out_shape = [jax.ShapeDtypeStruct((M, H * V), BF16)]
    out_specs = [pl.BlockSpec((T, V), lambda b, h: (b, h))]
    if emit_state:
        out_shape.append(jax.ShapeDtypeStruct((B, 2, H, K, V), F32))
        out_specs.append(pl.BlockSpec((None, 2, None, K, V), lambda b, h: (b, 0, h, 0, 0)))
    kern = functools.partial(
        _scan_kernel, T=T, K=K, V=V, is_gla=is_gla, use_rope=rope is not None, has_s0=s0 is not None,
        emit_state=emit_state, q_scale=(K ** -0.5 if is_gla else 1.0), k_scale=(1.0 if is_gla else K ** -0.5))
    return pl.pallas_call(
        kern,
        grid=(B, H),
        in_specs=in_specs,
        out_specs=out_specs,
        out_shape=out_shape,
        scratch_shapes=[pltpu.VMEM((2, K, V), F32), pltpu.VMEM((T, V), F32)],
        compiler_params=_cparams(("arbitrary", "arbitrary")),
        name="scan_gla" if is_gla else "scan_ret",
    )(*args)


def _attn_kernel(*refs, T, tq, tk, L, use_rope, lam_init):
    refs = list(refs)
    q_ref, k_ref, v_ref, lam_ref, g_ref = (refs.pop(0) for _ in range(5))
    if L:
        ck_ref, cv_ref = refs.pop(0), refs.pop(0)
    if use_rope:
        cos_ref, sin_ref = refs.pop(0), refs.pop(0)
    o_ref = refs.pop(0)
    if use_rope:
        kr_ref = refs.pop(0)
    Dh = DA_DH
    qi = pl.program_id(2)
    scale = Dh ** -0.5 * math.log2(math.e)

    def rope(x, rows):
        cs, sn = cos_ref[rows, :], sin_ref[rows, :]
        return jnp.concatenate(
            [x[:, m * Dh:(m + 1) * Dh] * cs + pltpu.roll(x[:, m * Dh:(m + 1) * Dh], Dh // 2, axis=1) * sn
             for m in range(2)], axis=-1)

    if use_rope:
        @pl.when(qi == 0)
        def _():
            def body(j, carry):
                rows = pl.ds(pl.multiple_of(j * tk, tk), tk)
                kr_ref[rows, :] = rope(k_ref[rows, :].astype(F32), rows).astype(BF16)
                return carry
            lax.fori_loop(0, T // tk, body, 0)

    qrows = pl.ds(pl.multiple_of(qi * tq, tq), tq)
    q = q_ref[...].astype(F32)
    if use_rope:
        q = rope(q, qrows)
    q = (q * scale).astype(BF16)
    qs = (q[:, :Dh], q[:, Dh:])

    def lane_fold(x, op):
        parts = [x[:, c:c + LANES] for c in range(0, x.shape[1], LANES)]
        while len(parts) > 1:
            parts = [op(parts[i], parts[i + 1]) for i in range(0, len(parts) - 1, 2)] + (
                [parts[-1]] if len(parts) % 2 else [])
        return parts[0]

    def update(carry, kblk, vblk):
        new = []
        for m in range(2):
            mx, l, acc = carry[m]
            s = _dot_nt(qs[m], kblk[:, m * Dh:(m + 1) * Dh])
            mx_new = jnp.maximum(mx, jnp.max(lane_fold(s, jnp.maximum), axis=-1, keepdims=True))
            alpha = jnp.exp2(mx - mx_new)
            p = jnp.exp2(s - mx_new)
            l = alpha * l + jnp.sum(lane_fold(p, jnp.add), axis=-1, keepdims=True)
            acc = alpha * acc + _dot(p.astype(BF16), vblk)
            new.append((mx_new, l, acc))
        return tuple(new)

    carry = tuple((jnp.full((tq, 1), -jnp.inf, F32), jnp.zeros((tq, 1), F32), jnp.zeros((tq, 2 * Dh), F32))
                  for _ in range(2))
    ksrc = kr_ref if use_rope else k_ref
    for j in range(T // tk):
        carry = update(carry, ksrc[j * tk:(j + 1) * tk, :], v_ref[j * tk:(j + 1) * tk, :])
    if L:
        carry = update(carry, ck_ref[...], cv_ref[...])

    lp = lam_ref[...]
    lam = (jnp.exp(jnp.sum(lp[0:1] * lp[1:2], axis=-1, keepdims=True))
           - jnp.exp(jnp.sum(lp[2:3] * lp[3:4], axis=-1, keepdims=True)) + lam_init)
    (_, l1, a1), (_, l2, a2) = carry
    o = a1 / l1 - lam * (a2 / l2)
    o_ref[...] = (_rms(o) * g_ref[...] * (1.0 - lam_init)).astype(o_ref.dtype)


def diff_attention(z, *, T, lam_p, subln_g, lam_init, ctx=None, rope=None, tq=512, tk=1024):
    M = z.shape[0]
    B = M // T
    H, W = DA_H, 2 * DA_DH
    tq, tk = min(tq, T), min(tk, T)
    nq = T // tq
    args = [z, z, z, lam_p, subln_g.reshape(1, W)]
    in_specs = [
        pl.BlockSpec((tq, W), lambda b, h, i: (b * nq + i, h)),
        pl.BlockSpec((T, W), lambda b, h, i: (b, H + h)),
        pl.BlockSpec((T, W), lambda b, h, i: (b, 2 * H + h)),
        pl.BlockSpec((4, DA_DH), lambda b, h, i: (0, 0)),
        pl.BlockSpec((1, W), lambda b, h, i: (0, 0)),
    ]
    L = 0
    if ctx is not None:
        L = ctx[0].shape[1]
        args += [ctx[0], ctx[1]]
        in_specs += [pl.BlockSpec((None, L, W), lambda b, h, i: (b, 0, h))] * 2
    scratch = []
    if rope is not None:
        args += [rope[0], rope[1]]
        in_specs += [pl.BlockSpec((T, DA_DH), lambda b, h, i: (0, 0))] * 2
        scratch.append(pltpu.VMEM((T, W), BF16))
    return pl.pallas_call(
        functools.partial(_attn_kernel, T=T, tq=tq, tk=tk, L=L, use_rope=rope is not None, lam_init=lam_init),
        grid=(B, H, nq),
        in_specs=in_specs,
        out_specs=pl.BlockSpec((tq, W), lambda b, h, i: (b * nq + i, h)),
        out_shape=jax.ShapeDtypeStruct((M, H * W), BF16),
        scratch_shapes=scratch,
        compiler_params=_cparams(("arbitrary", "arbitrary", "arbitrary")),
        name="diff_attention",
    )(*args)


def _rope_tables(n_tokens, head_dim):
    n_rows = n_tokens // GRID_W
    row = jnp.repeat(jnp.arange(n_rows), GRID_W).astype(F32)
    col = jnp.tile(jnp.arange(GRID_W), n_rows).astype(F32)
    quarter = head_dim // 4
    inv = ROPE_BASE ** (-jnp.arange(quarter, dtype=F32) / quarter)
    ang = jnp.concatenate([row[:, None] * inv, col[:, None] * inv], axis=-1)
    return jnp.cos(ang), jnp.sin(ang)


def _pad_cols(a, n):
    return jnp.pad(a, [(0, 0)] * (a.ndim - 1) + [(0, n - a.shape[-1])])


def kernel(x_prompt, x_sample, state_ret, state_gla, cache_k, cache_v, c, c_ctx, w_mod, b_mod, norm_g,
           w_even_in, w_even_out, ret_decay, ret_norm_g, gla_w2, gla_b, gla_norm_g,
           w_odd_in, w_odd_out, da_lambda, da_subln_g, w_ffn_in, ffn_conv_w, ffn_conv_b, w_ffn_out):
    Bp, Tp, D = x_prompt.shape
    Bs, Ts, _ = x_sample.shape
    depth = w_mod.shape[0]
    F = w_ffn_out.shape[1]
    Fp = -(-F // 1024) * 1024
    ret_qkw, ret_vw = RET_H * RET_DK, RET_H * RET_DV
    gla_kw, gla_vw = GLA_H * GLA_DK, GLA_H * GLA_DV
    even_main = 2 * ret_qkw + 2 * ret_vw + 2 * gla_kw + 2 * gla_vw

    cond = jnp.zeros((MOD_ROWS, D), F32).at[0].set(c_ctx).at[1:1 + Bs].set(c)
    mods = mod_matmul(cond, w_mod, b_mod).reshape(depth, MOD_ROWS, 1, N_MOD * D)

    ret_rope = _rope_tables(Ts, RET_DK)
    da_cos, da_sin = _rope_tables(Ts, DA_DH)
    da_rope = (jnp.concatenate([da_cos, da_cos], axis=-1), jnp.concatenate([-da_sin, da_sin], axis=-1))

    streams = [dict(T=Tp, g0=0, rpg=Bp * Tp), dict(T=Ts, g0=1, rpg=Ts)]
    ys = [x_prompt.reshape(Bp * Tp, D), x_sample.reshape(Bs * Ts, D)]
    hs = [rowwise(y, mods, st["g0"], st["rpg"], mod=(norm_g[0, 0], 0, 0, 1))[0] for y, st in zip(ys, streams)]

    ret_states, gla_states, ks_out, vs_out = [], [], [], []
    for l in range(depth):
        if l % 2 == 0:
            e = l // 2
            w_main = w_even_in[e, :, :even_main].astype(BF16)
            w_rank = _pad_cols(w_even_in[e, :, even_main:], LANES).astype(BF16)
            w_out = w_even_out[e].astype(BF16)
            dec = jnp.broadcast_to(ret_decay[e].T[:, :, None, None], (RET_H, 2, 8, LANES))
            w2 = gla_w2[e].reshape(2, GLA_RANK, GLA_H, GLA_DK).transpose(2, 0, 1, 3)
            w2p = jnp.zeros((GLA_H, 2, LANES, GLA_DK), F32)
            w2p = w2p.at[:, 0, :GLA_RANK].set(w2[:, 0]).at[:, 1, GLA_RANK:2 * GLA_RANK].set(w2[:, 1]).astype(BF16)
            gbias = gla_b[e].reshape(2, GLA_H, 1, GLA_DK).transpose(1, 0, 2, 3)
            mix = []
            for si, st in enumerate(streams):
                z = matmul(hs[si], w_main, BF16)
                zr = matmul(hs[si], w_rank, BF16)
                lat = si == 1
                s0r = (state_ret, e) if lat else None
                s0g = (state_gla, e) if lat else None
                ro = scan_mixer(z, None, T=st["T"], col_q=0, col_k=ret_qkw, col_v=2 * ret_qkw,
                                col_gate=2 * ret_qkw + ret_vw, H=RET_H, K=RET_DK, V=RET_DV,
                                norm_g=ret_norm_g[e], is_gla=False, dec=dec,
                                rope=ret_rope if lat else None, s0=s0r, emit_state=not lat)
                g0 = 2 * ret_qkw + 2 * ret_vw
                go = scan_mixer(z, zr, T=st["T"], col_q=g0, col_k=g0 + gla_kw, col_v=g0 + 2 * gla_kw,
                                col_gate=g0 + 2 * gla_kw + gla_vw, H=GLA_H, K=GLA_DK, V=GLA_DV,
                                norm_g=gla_norm_g[e], is_gla=True, w2=w2p, gbias=gbias,
                                s0=s0g, emit_state=not lat)
                if not lat:
                    ret_states.append(ro[1])
                    gla_states.append(go[1])
                mix.append(matmul_pair(ro[0], go[0], w_out, BF16))
        else:
            o = l // 2
            lam_init = 0.8 - 0.6 * math.exp(-0.3 * l)
            w_in = w_odd_in[o].astype(BF16)
            w_out = w_odd_out[o].astype(BF16)
            ctx = (cache_k[:, o].reshape(Bs, -1, DA_H * 2 * DA_DH).astype(BF16),
                   cache_v[:, o].reshape(Bs, -1, DA_H * 2 * DA_DH).astype(BF16))
            mix = []
            for si, st in enumerate(streams):
                z = matmul(hs[si], w_in, BF16)
                lat = si == 1
                a = diff_attention(z, T=st["T"], lam_p=da_lambda[o], subln_g=da_subln_g[o], lam_init=lam_init,
                                   ctx=ctx if lat else None, rope=da_rope if lat else None)
                if not lat:
                    kv_w = DA_H * 2 * DA_DH
                    ks_out.append(z[:, kv_w:2 * kv_w].astype(F32).reshape(Bp, Tp, DA_H, 2 * DA_DH))
                    vs_out.append(z[:, 2 * kv_w:].astype(F32).reshape(Bp, Tp, DA_H, 2 * DA_DH))
                mix.append(matmul(a, w_out, BF16))

        wf = w_ffn_in[l].reshape(D, 2, F)
        wg = _pad_cols(wf[:, 0], Fp).astype(BF16)
        wv = _pad_cols(wf[:, 1], Fp).astype(BF16)
        cw = _pad_cols(ffn_conv_w[l].reshape(3, 2, F).transpose(1, 0, 2), Fp)
        cb = _pad_cols(ffn_conv_b[l].reshape(2, 1, F), Fp)
        wo = jnp.pad(w_ffn_out[l], ((0, Fp - F), (0, 0))).astype(BF16)
        for si, st in enumerate(streams):
            ys[si], h2 = rowwise(ys[si], mods, st["g0"], st["rpg"],
                                 resid=(mix[si], norm_g[l, 1], l, 2), mod=(norm_g[l, 2], l, 3, 4))
            act = ffn_in(h2, wg, wv, cw, cb, st["T"])
            f = matmul(act, wo, BF16, tk=Fp // 4)
            nxt = (norm_g[l + 1, 0], l + 1, 0, 1) if l + 1 < depth else None
            res = rowwise(ys[si], mods, st["g0"], st["rpg"], resid=(f, norm_g[l, 3], l, 5), mod=nxt)
            ys[si] = res[0]
            hs[si] = res[1] if nxt is not None else None

    new_state_ret = jnp.stack(ret_states, axis=1)
    new_state_gla = jnp.stack(gla_states, axis=1)
    new_cache_k = jnp.stack(ks_out, axis=1)
    new_cache_v = jnp.stack(vs_out, axis=1)
    return (ys[0].reshape(Bp, Tp, D), ys[1].reshape(Bs, Ts, D),
            new_state_ret, new_state_gla, new_cache_k, new_cache_v)
```

```python
import functools
import math

import jax
import jax.numpy as jnp
from jax import lax
from jax.experimental import pallas as pl
from jax.experimental.pallas import tpu as pltpu

F32 = jnp.float32
BF16 = jnp.bfloat16

V7X_VMEM_LIMIT_BYTES = 58 * 1024 * 1024
LANES = 128
HALO = 8

GRID_W = 64
RET_H, RET_DK, RET_DV = 8, 256, 256
GLA_H, GLA_DK, GLA_DV = 4, 256, 512
GLA_RANK = 16
GLA_TAU = 16.0
DA_H, DA_DH = 16, 128
CHUNK = 64
ROPE_BASE = 10000.0
EPS = 1e-6
N_MOD = 6
MOD_ROWS = 16


def _cparams(sem):
    return pltpu.CompilerParams(dimension_semantics=sem, vmem_limit_bytes=V7X_VMEM_LIMIT_BYTES)


def _dot(a, b):
    return jnp.dot(a, b, preferred_element_type=F32)


def _dot_nt(a, b):
    return lax.dot_general(a, b, (((1,), (1,)), ((), ())), preferred_element_type=F32)


def _dot_tn(a, b):
    return lax.dot_general(a, b, (((0,), (0,)), ((), ())), preferred_element_type=F32)


def _silu(x):
    return x * (1.0 / (1.0 + jnp.exp(-x)))


def _mod_kernel(c_ref, w_ref, b_ref, o_ref):
    s = _silu(c_ref[...]).astype(BF16)
    o_ref[...] = _dot(s, w_ref[...].astype(BF16)) + b_ref[...]


def mod_matmul(cond, w_mod, b_mod, tn=1024):
    L, D, N = w_mod.shape
    assert N % tn == 0, (N, tn)
    return pl.pallas_call(
        _mod_kernel,
        grid=(L, N // tn),
        in_specs=[
            pl.BlockSpec((MOD_ROWS, D), lambda l, j: (0, 0)),
            pl.BlockSpec((None, D, tn), lambda l, j: (l, 0, j)),
            pl.BlockSpec((None, 1, tn), lambda l, j: (l, 0, j)),
        ],
        out_specs=pl.BlockSpec((None, MOD_ROWS, tn), lambda l, j: (l, 0, j)),
        out_shape=jax.ShapeDtypeStruct((L, MOD_ROWS, N), F32),
        compiler_params=_cparams(("arbitrary", "arbitrary")),
        name="mod_matmul",
    )(cond, w_mod, b_mod.reshape(L, 1, N))


def _rms(x):
    return x * lax.rsqrt(jnp.mean(x * x, axis=-1, keepdims=True) + EPS)


def _rowwise_kernel(*refs, has_resid, has_mod):
    refs = list(refs)
    y_ref = refs.pop(0)
    y = y_ref[...]
    if has_resid:
        o_ref, gpost_ref, gate_ref = refs.pop(0), refs.pop(0), refs.pop(0)
        y = y + gate_ref[...] * (_rms(o_ref[...].astype(F32)) * gpost_ref[...])
    if has_mod:
        gpre_ref, shift_ref, scale_ref = refs.pop(0), refs.pop(0), refs.pop(0)
    if has_resid:
        ynew_ref = refs.pop(0)
        ynew_ref[...] = y
    if has_mod:
        h_ref = refs.pop(0)
        h = (_rms(y) * gpre_ref[...]) * (1.0 + scale_ref[...]) + shift_ref[...]
        h_ref[...] = h.astype(BF16)


def rowwise(y, mods, group0, rows_per_group, *, resid=None, mod=None, tr=256):
    M, D = y.shape
    tr = min(tr, M)

    def mod_spec(layer, chunk):
        return pl.BlockSpec((None, None, 1, D),
                            lambda i: (layer, group0 + (i * tr) // rows_per_group, 0, chunk))

    row_spec = pl.BlockSpec((tr, D), lambda i: (i, 0))
    vec_spec = pl.BlockSpec((1, D), lambda i: (0, 0))
    args, in_specs = [y], [row_spec]
    out_shape, out_specs = [], []
    if resid is not None:
        o, g_post, layer, gate_chunk = resid
        args += [o, g_post.reshape(1, D), mods]
        in_specs += [row_spec, vec_spec, mod_spec(layer, gate_chunk)]
        out_shape.append(jax.ShapeDtypeStruct((M, D), F32))
        out_specs.append(row_spec)
    if mod is not None:
        g_pre, layer, shift_chunk, scale_chunk = mod
        args += [g_pre.reshape(1, D), mods, mods]
        in_specs += [vec_spec, mod_spec(layer, shift_chunk), mod_spec(layer, scale_chunk)]
        out_shape.append(jax.ShapeDtypeStruct((M, D), BF16))
        out_specs.append(row_spec)
    return pl.pallas_call(
        functools.partial(_rowwise_kernel, has_resid=resid is not None, has_mod=mod is not None),
        grid=(M // tr,),
        in_specs=in_specs,
        out_specs=out_specs,
        out_shape=out_shape,
        compiler_params=_cparams(("arbitrary",)),
        name="rowwise",
    )(*args)


def _mm_kernel(x_ref, w_ref, o_ref):
    o_ref[...] = _dot(x_ref[...], w_ref[...]).astype(o_ref.dtype)


def _mm_acc_kernel(x_ref, w_ref, o_ref, acc_ref):
    k = pl.program_id(2)

    @pl.when(k == 0)
    def _():
        acc_ref[...] = jnp.zeros_like(acc_ref)

    acc_ref[...] += _dot(x_ref[...], w_ref[...])

    @pl.when(k == pl.num_programs(2) - 1)
    def _():
        o_ref[...] = acc_ref[...].astype(o_ref.dtype)


def matmul(x, w, out_dtype, tm=1024, tn=1024, tk=None):
    M, K = x.shape
    N = w.shape[1]
    tm, tn = min(tm, M), min(tn, N)
    if tk is None or tk >= K:
        return pl.pallas_call(
            _mm_kernel,
            grid=(M // tm, N // tn),
            in_specs=[pl.BlockSpec((tm, K), lambda i, j: (i, 0)),
                      pl.BlockSpec((K, tn), lambda i, j: (0, j))],
            out_specs=pl.BlockSpec((tm, tn), lambda i, j: (i, j)),
            out_shape=jax.ShapeDtypeStruct((M, N), out_dtype),
            compiler_params=_cparams(("arbitrary", "arbitrary")),
            name="matmul",
        )(x, w)
    return pl.pallas_call(
        _mm_acc_kernel,
        grid=(M // tm, N // tn, K // tk),
        in_specs=[pl.BlockSpec((tm, tk), lambda i, j, k: (i, k)),
                  pl.BlockSpec((tk, tn), lambda i, j, k: (k, j))],
        out_specs=pl.BlockSpec((tm, tn), lambda i, j, k: (i, j)),
        out_shape=jax.ShapeDtypeStruct((M, N), out_dtype),
        scratch_shapes=[pltpu.VMEM((tm, tn), F32)],
        compiler_params=_cparams(("arbitrary", "arbitrary", "arbitrary")),
        name="matmul_kgrid",
    )(x, w)


def _mm_pair_kernel(xa_ref, xb_ref, wa_ref, wb_ref, o_ref):
    o_ref[...] = (_dot(xa_ref[...], wa_ref[...]) + _dot(xb_ref[...], wb_ref[...])).astype(o_ref.dtype)


def matmul_pair(xa, xb, w, out_dtype, tm=1024, tn=1024):
    M, Kh = xa.shape
    N = w.shape[1]
    tm, tn = min(tm, M), min(tn, N)
    return pl.pallas_call(
        _mm_pair_kernel,
        grid=(M // tm, N // tn),
        in_specs=[pl.BlockSpec((tm, Kh), lambda i, j: (i, 0)),
                  pl.BlockSpec((tm, Kh), lambda i, j: (i, 0)),
                  pl.BlockSpec((Kh, tn), lambda i, j: (0, j)),
                  pl.BlockSpec((Kh, tn), lambda i, j: (1, j))],
        out_specs=pl.BlockSpec((tm, tn), lambda i, j: (i, j)),
        out_shape=jax.ShapeDtypeStruct((M, N), out_dtype),
        compiler_params=_cparams(("arbitrary", "arbitrary")),
        name="matmul_pair",
    )(xa, xb, w, w)


def _ffn_in_kernel(h_ref, hprev_ref, hnext_ref, wg_ref, wv_ref, cw_ref, cb_ref, o_ref, hext_ref,
                   *, tm, tfs, seq_len):
    i = pl.program_id(0)

    @pl.when(pl.program_id(1) == 0)
    def _():
        hext_ref[0:HALO, :] = hprev_ref[...]
        hext_ref[HALO:HALO + tm, :] = h_ref[...]
        hext_ref[HALO + tm:, :] = hnext_ref[...]

    hext = hext_ref[...]
    rows = tm + 2 * HALO
    pos = (i * tm + lax.broadcasted_iota(jnp.int32, (tm, 1), 0)) % seq_len
    not_first = (pos != 0).astype(F32)
    not_last = (pos != seq_len - 1).astype(F32)

    def conv(w_ref, half, cols):
        u = _dot(hext, w_ref[:, cols])
        u_prev = pltpu.roll(u, 1, axis=0)[HALO:HALO + tm]
        u_next = pltpu.roll(u, rows - 1, axis=0)[HALO:HALO + tm]
        cw = cw_ref[half][:, cols]
        return (u_prev * not_first * cw[0:1] + u[HALO:HALO + tm] * cw[1:2]
                + u_next * not_last * cw[2:3] + cb_ref[half][:, cols])

    for c0 in range(0, o_ref.shape[1], tfs):
        cols = slice(c0, c0 + tfs)
        g = conv(wg_ref, 0, cols)
        val = conv(wv_ref, 1, cols)
        o_ref[:, cols] = (_silu(g) * val).astype(o_ref.dtype)


def ffn_in(h, wg, wv, cw, cb, seq_len, tm=1024, tf=512, tfs=512):
    M, D = h.shape
    Fp = wg.shape[1]
    tm, tf = min(tm, M), min(tf, Fp)
    tfs = min(tfs, tf)
    nb = tm // HALO
    last = M // HALO - 1
    return pl.pallas_call(
        functools.partial(_ffn_in_kernel, tm=tm, tfs=tfs, seq_len=seq_len),
        grid=(M // tm, Fp // tf),
        in_specs=[
            pl.BlockSpec((tm, D), lambda i, j: (i, 0)),
            pl.BlockSpec((HALO, D), lambda i, j: (jnp.maximum(i * nb - 1, 0), 0)),
            pl.BlockSpec((HALO, D), lambda i, j: (jnp.minimum((i + 1) * nb, last), 0)),
            pl.BlockSpec((D, tf), lambda i, j: (0, j)),
            pl.BlockSpec((D, tf), lambda i, j: (0, j)),
            pl.BlockSpec((2, 3, tf), lambda i, j: (0, 0, j)),
            pl.BlockSpec((2, 1, tf), lambda i, j: (0, 0, j)),
        ],
        out_specs=pl.BlockSpec((tm, tf), lambda i, j: (i, j)),
        out_shape=jax.ShapeDtypeStruct((M, Fp), BF16),
        scratch_shapes=[pltpu.VMEM((tm + 2 * HALO, D), BF16)],
        compiler_params=_cparams(("arbitrary", "arbitrary")),
        name="ffn_in",
    )(h, h, h, wg, wv, cw, cb)


def _scan_kernel(*refs, T, K, V, is_gla, use_rope, has_s0, emit_state, q_scale, k_scale):
    refs = list(refs)
    q_ref, k_ref, v_ref, gate_ref, ng_ref = (refs.pop(0) for _ in range(5))
    if is_gla:
        r_ref, w2_ref, gb_ref = (refs.pop(0) for _ in range(3))
    else:
        dec_ref = refs.pop(0)
    if use_rope:
        cos_ref, sin_ref = refs.pop(0), refs.pop(0)
    if has_s0:
        s0_ref = refs.pop(0)
    out_ref = refs.pop(0)
    if emit_state:
        sfin_ref = refs.pop(0)
    s_ref, osum_ref = refs.pop(0), refs.pop(0)
    if is_gla:
        b_ref = refs.pop(0)

    C = CHUNK
    n = T // C
    half = K // 2
    ti = lax.broadcasted_iota(jnp.int32, (C, C), 0)
    si = lax.broadcasted_iota(jnp.int32, (C, C), 1)
    tri = (ti >= si, si >= ti)
    trow = lax.broadcasted_iota(jnp.int32, (C, 1), 0)
    tcol = trow.astype(F32)

    for d in range(2):
        if has_s0:
            s_ref[d] = s0_ref[d]
        else:
            s_ref[d] = jnp.zeros((K, V), F32)

    if not is_gla:
        lg = tuple(-jnp.exp(dec_ref[d][0:1, 0:1]) for d in range(2))
        dmat = (jnp.where(tri[0], jnp.exp((ti - si).astype(F32) * lg[0]), 0.0),
                jnp.where(tri[1], jnp.exp((si - ti).astype(F32) * lg[1]), 0.0))
        e_in = (jnp.exp((tcol + 1.0) * lg[0]), jnp.exp((C - tcol) * lg[1]))
        e_out = (jnp.exp((C - 1.0 - tcol) * lg[0]), jnp.exp(tcol * lg[1]))
        e_tot = (jnp.exp(C * lg[0]), jnp.exp(C * lg[1]))

    if is_gla:
        def cum_body(c, carry):
            rows = pl.ds(pl.multiple_of(c * C, C), C)
            for d in range(2):
                logits = _dot(r_ref[rows, :], w2_ref[d]) + gb_ref[d]
                la = (jnp.minimum(logits, 0.0) - jnp.log(1.0 + jnp.exp(-jnp.abs(logits)))) * (1.0 / GLA_TAU)
                b = la
                sh = 1
                while sh < C:
                    if d == 0:
                        b = b + jnp.where(trow >= sh, pltpu.roll(b, sh, axis=0), 0.0)
                    else:
                        b = b + jnp.where(trow < C - sh, pltpu.roll(b, C - sh, axis=0), 0.0)
                    sh *= 2
                b_ref[d, rows, :] = b
            return carry
        lax.fori_loop(0, n, cum_body, 0, unroll=min(n, 4))

    def load_qk(ref, rows, scale):
        x = ref[rows, :].astype(F32)
        if use_rope:
            cs, sn = cos_ref[rows, :], sin_ref[rows, :]
            x1, x2 = x[:, :half], x[:, half:]
            x = jnp.concatenate([x1 * cs - x2 * sn, x1 * sn + x2 * cs], axis=-1)
        if scale != 1.0:
            x = x * scale
        return x

    def chunk_step(c, d):
        cc = c if d == 0 else n - 1 - c
        rows = pl.ds(pl.multiple_of(cc * C, C), C)
        qc = load_qk(q_ref, rows, q_scale)
        kc = load_qk(k_ref, rows, k_scale)
        vc = v_ref[rows, :]
        s_old = s_ref[d]
        if is_gla:
            b = b_ref[d, rows, :]
            b_end = b[C - 1:C, :] if d == 0 else b[0:1, :]
            q_in = (qc * jnp.exp(b)).astype(BF16)
            k_in = (kc * jnp.exp(-b)).astype(BF16)
            k_out = kc * jnp.exp(b_end - b)
            att = jnp.where(tri[d], _dot_nt(q_in, k_in), 0.0)
            s_decay = jnp.exp(jnp.broadcast_to(b_end, (LANES, K)).T[:, 0:1])
        else:
            att = _dot_nt(qc.astype(BF16), kc.astype(BF16)) * dmat[d]
            q_in = (qc * e_in[d]).astype(BF16)
            k_out = kc * e_out[d]
            s_decay = e_tot[d]
        o = _dot(att.astype(BF16), vc) + _dot(q_in, s_old.astype(BF16))
        s_ref[d] = s_decay * s_old + _dot(k_out.T.astype(BF16), vc)
        return rows, o

    def first_half(c, carry):
        for d in range(2):
            rows, o = chunk_step(c, d)
            osum_ref[rows, :] = o
        return carry

    def second_half(c, carry):
        for d in range(2):
            rows, o = chunk_step(c, d)
            o = o + osum_ref[rows, :]
            gate = gate_ref[rows, :].astype(F32)
            out_ref[rows, :] = (_rms(o) * ng_ref[...] * _silu(gate)).astype(out_ref.dtype)
        return carry

    unroll = min(n // 2, 2 if is_gla else 8)
    lax.fori_loop(0, n // 2, first_half, 0, unroll=unroll)
    lax.fori_loop(n // 2, n, second_half, 0, unroll=unroll)
    if emit_state:
        for d in range(2):
            sfin_ref[d] = s_ref[d]


def scan_mixer(z, zr, *, T, col_q, col_k, col_v, col_gate, H, K, V, norm_g, is_gla,
               dec=None, w2=None, gbias=None, rope=None, s0=None, emit_state=False):
    M = z.shape[0]
    B = M // T
    args = [z, z, z, z, norm_g.reshape(1, H * V)]
    in_specs = [
        pl.BlockSpec((T, K), lambda b, h: (b, col_q // K + h)),
        pl.BlockSpec((T, K), lambda b, h: (b, col_k // K + h)),
        pl.BlockSpec((T, V), lambda b, h: (b, col_v // V + h)),
        pl.BlockSpec((T, V), lambda b, h: (b, col_gate // V + h)),
        pl.BlockSpec((1, V), lambda b, h: (0, h)),
    ]
    if is_gla:
        args += [zr, w2, gbias]
        in_specs += [pl.BlockSpec((T, LANES), lambda b, h: (b, 0)),
                     pl.BlockSpec((None, 2, LANES, K), lambda b, h: (h, 0, 0, 0)),
                     pl.BlockSpec((None, 2, 1, K), lambda b, h: (h, 0, 0, 0))]
    else:
        args += [dec]
        in_specs += [pl.BlockSpec((None, 2, 8, LANES), lambda b, h: (h, 0, 0, 0))]
    if rope is not None:
        args += [rope[0], rope[1]]
        in_specs += [pl.BlockSpec((T, K // 2), lambda b, h: (0, 0))] * 2
    if s0 is not None:
        s0_arr, s0_e = s0
        args += [s0_arr]
        in_specs += [pl.BlockSpec((None, None, 2, None, K, V), lambda b, h: (b, s0_e, 0, h, 0, 0))]
    out_shape = [jax.ShapeDtypeStruct((M, H * V), BF16)]
    out_specs = [pl.BlockSpec((T, V), lambda b, h: (b, h))]
    if emit_state:
        out_shape.append(jax.ShapeDtypeStruct((B, 2, H, K, V), F32))
        out_specs.append(pl.BlockSpec((None, 2, None, K, V), lambda b, h: (b, 0, h, 0, 0)))
    kern = functools.partial(
        _scan_kernel, T=T, K=K, V=V, is_gla=is_gla, use_rope=rope is not None, has_s0=s0 is not None,
        emit_state=emit_state, q_scale=(K ** -0.5 if is_gla else 1.0), k_scale=(1.0 if is_gla else K ** -0.5))
    return pl.pallas_call(
        kern,
        grid=(B, H),
        in_specs=in_specs,
        out_specs=out_specs,
        out_shape=out_shape,
        scratch_shapes=[pltpu.VMEM((2, K, V), F32), pltpu.VMEM((T, V), F32)]
        + ([pltpu.VMEM((2, T, K), F32)] if is_gla else []),
        compiler_params=_cparams(("arbitrary", "arbitrary")),
        name="scan_gla" if is_gla else "scan_ret",
    )(*args)


def _attn_kernel(*refs, T, tq, tk, L, use_rope, lam_init):
    refs = list(refs)
    q_ref, k_ref, v_ref, lam_ref, g_ref = (refs.pop(0) for _ in range(5))
    if L:
        ck_ref, cv_ref = refs.pop(0), refs.pop(0)
    if use_rope:
        cos_ref, sin_ref = refs.pop(0), refs.pop(0)
    o_ref = refs.pop(0)
    if use_rope:
        kr_ref = refs.pop(0)
    Dh = DA_DH
    qi = pl.program_id(2)
    scale = Dh ** -0.5 * math.log2(math.e)

    def rope(x, rows):
        cs, sn = cos_ref[rows, :], sin_ref[rows, :]
        return jnp.concatenate(
            [x[:, m * Dh:(m + 1) * Dh] * cs + pltpu.roll(x[:, m * Dh:(m + 1) * Dh], Dh // 2, axis=1) * sn
             for m in range(2)], axis=-1)

    if use_rope:
        @pl.when(qi == 0)
        def _():
            def body(j, carry):
                rows = pl.ds(pl.multiple_of(j * tk, tk), tk)
                kr_ref[rows, :] = rope(k_ref[rows, :].astype(F32), rows).astype(BF16)
                return carry
            lax.fori_loop(0, T // tk, body, 0)

    qrows = pl.ds(pl.multiple_of(qi * tq, tq), tq)
    q = q_ref[...].astype(F32)
    if use_rope:
        q = rope(q, qrows)
    q = (q * scale).astype(BF16)
    qs = (q[:, :Dh], q[:, Dh:])

    def lane_fold(x, op):
        parts = [x[:, c:c + LANES] for c in range(0, x.shape[1], LANES)]
        while len(parts) > 1:
            parts = [op(parts[i], parts[i + 1]) for i in range(0, len(parts) - 1, 2)] + (
                [parts[-1]] if len(parts) % 2 else [])
        return parts[0]

    def update(carry, kblk, vblk):
        new = []
        for m in range(2):
            mx, l, acc = carry[m]
            s = _dot_nt(qs[m], kblk[:, m * Dh:(m + 1) * Dh])
            mx_new = jnp.maximum(mx, jnp.max(lane_fold(s, jnp.maximum), axis=-1, keepdims=True))
            alpha = jnp.exp2(mx - mx_new)
            p = jnp.exp2(s - mx_new)
            l = alpha * l + jnp.sum(lane_fold(p, jnp.add), axis=-1, keepdims=True)
            acc = alpha * acc + _dot(p.astype(BF16), vblk)
            new.append((mx_new, l, acc))
        return tuple(new)

    carry = tuple((jnp.full((tq, 1), -jnp.inf, F32), jnp.zeros((tq, 1), F32), jnp.zeros((tq, 2 * Dh), F32))
                  for _ in range(2))
    ksrc = kr_ref if use_rope else k_ref
    for j in range(T // tk):
        carry = update(carry, ksrc[j * tk:(j + 1) * tk, :], v_ref[j * tk:(j + 1) * tk, :])
    if L:
        carry = update(carry, ck_ref[...], cv_ref[...])

    lp = lam_ref[...]
    lam = (jnp.exp(jnp.sum(lp[0:1] * lp[1:2], axis=-1, keepdims=True))
           - jnp.exp(jnp.sum(lp[2:3] * lp[3:4], axis=-1, keepdims=True)) + lam_init)
    (_, l1, a1), (_, l2, a2) = carry
    o = a1 / l1 - lam * (a2 / l2)
    o_ref[...] = (_rms(o) * g_ref[...] * (1.0 - lam_init)).astype(o_ref.dtype)


def diff_attention(z, *, T, lam_p, subln_g, lam_init, ctx=None, rope=None, tq=512, tk=1024):
    M = z.shape[0]
    B = M // T
    H, W = DA_H, 2 * DA_DH
    tq, tk = min(tq, T), min(tk, T)
    nq = T // tq
    args = [z, z, z, lam_p, subln_g.reshape(1, W)]
    in_specs = [
        pl.BlockSpec((tq, W), lambda b, h, i: (b * nq + i, h)),
        pl.BlockSpec((T, W), lambda b, h, i: (b, H + h)),
        pl.BlockSpec((T, W), lambda b, h, i: (b, 2 * H + h)),
        pl.BlockSpec((4, DA_DH), lambda b, h, i: (0, 0)),
        pl.BlockSpec((1, W), lambda b, h, i: (0, 0)),
    ]
    L = 0
    if ctx is not None:
        L = ctx[0].shape[1]
        args += [ctx[0], ctx[1]]
        in_specs += [pl.BlockSpec((None, L, W), lambda b, h, i: (b, 0, h))] * 2
    scratch = []
    if rope is not None:
        args += [rope[0], rope[1]]
        in_specs += [pl.BlockSpec((T, DA_DH), lambda b, h, i: (0, 0))] * 2
        scratch.append(pltpu.VMEM((T, W), BF16))
    return pl.pallas_call(
        functools.partial(_attn_kernel, T=T, tq=tq, tk=tk, L=L, use_rope=rope is not None, lam_init=lam_init),
        grid=(B, H, nq),
        in_specs=in_specs,
        out_specs=pl.BlockSpec((tq, W), lambda b, h, i: (b * nq + i, h)),
        out_shape=jax.ShapeDtypeStruct((M, H * W), BF16),
        scratch_shapes=scratch,
        compiler_params=_cparams(("arbitrary", "arbitrary", "arbitrary")),
        name="diff_attention",
    )(*args)


def _rope_tables(n_tokens, head_dim):
    n_rows = n_tokens // GRID_W
    row = jnp.repeat(jnp.arange(n_rows), GRID_W).astype(F32)
    col = jnp.tile(jnp.arange(GRID_W), n_rows).astype(F32)
    quarter = head_dim // 4
    inv = ROPE_BASE ** (-jnp.arange(quarter, dtype=F32) / quarter)
    ang = jnp.concatenate([row[:, None] * inv, col[:, None] * inv], axis=-1)
    return jnp.cos(ang), jnp.sin(ang)


def _pad_cols(a, n):
    return jnp.pad(a, [(0, 0)] * (a.ndim - 1) + [(0, n - a.shape[-1])])


def kernel(x_prompt, x_sample, state_ret, state_gla, cache_k, cache_v, c, c_ctx, w_mod, b_mod, norm_g,
           w_even_in, w_even_out, ret_decay, ret_norm_g, gla_w2, gla_b, gla_norm_g,
           w_odd_in, w_odd_out, da_lambda, da_subln_g, w_ffn_in, ffn_conv_w, ffn_conv_b, w_ffn_out):
    Bp, Tp, D = x_prompt.shape
    Bs, Ts, _ = x_sample.shape
    depth = w_mod.shape[0]
    F = w_ffn_out.shape[1]
    Fp = -(-F // 1024) * 1024
    ret_qkw, ret_vw = RET_H * RET_DK, RET_H * RET_DV
    gla_kw, gla_vw = GLA_H * GLA_DK, GLA_H * GLA_DV
    even_main = 2 * ret_qkw + 2 * ret_vw + 2 * gla_kw + 2 * gla_vw

    cond = jnp.zeros((MOD_ROWS, D), F32).at[0].set(c_ctx).at[1:1 + Bs].set(c)
    mods = mod_matmul(cond, w_mod, b_mod).reshape(depth, MOD_ROWS, 1, N_MOD * D)

    ret_rope = _rope_tables(Ts, RET_DK)
    da_cos, da_sin = _rope_tables(Ts, DA_DH)
    da_rope = (jnp.concatenate([da_cos, da_cos], axis=-1), jnp.concatenate([-da_sin, da_sin], axis=-1))

    streams = [dict(T=Tp, g0=0, rpg=Bp * Tp), dict(T=Ts, g0=1, rpg=Ts)]
    ys = [x_prompt.reshape(Bp * Tp, D), x_sample.reshape(Bs * Ts, D)]
    hs = [rowwise(y, mods, st["g0"], st["rpg"], mod=(norm_g[0, 0], 0, 0, 1))[0] for y, st in zip(ys, streams)]

    ret_states, gla_states, ks_out, vs_out = [], [], [], []
    for l in range(depth):
        if l % 2 == 0:
            e = l // 2
            w_main = w_even_in[e, :, :even_main].astype(BF16)
            w_rank = _pad_cols(w_even_in[e, :, even_main:], LANES).astype(BF16)
            w_out = w_even_out[e].astype(BF16)
            dec = jnp.broadcast_to(ret_decay[e].T[:, :, None, None], (RET_H, 2, 8, LANES))
            w2 = gla_w2[e].reshape(2, GLA_RANK, GLA_H, GLA_DK).transpose(2, 0, 1, 3)
            w2p = jnp.zeros((GLA_H, 2, LANES, GLA_DK), F32)
            w2p = w2p.at[:, 0, :GLA_RANK].set(w2[:, 0]).at[:, 1, GLA_RANK:2 * GLA_RANK].set(w2[:, 1]).astype(BF16)
            gbias = gla_b[e].reshape(2, GLA_H, 1, GLA_DK).transpose(1, 0, 2, 3)
            mix = []
            for si, st in enumerate(streams):
                z = matmul(hs[si], w_main, BF16)
                zr = matmul(hs[si], w_rank, BF16)
                lat = si == 1
                s0r = (state_ret, e) if lat else None
                s0g = (state_gla, e) if lat else None
                ro = scan_mixer(z, None, T=st["T"], col_q=0, col_k=ret_qkw, col_v=2 * ret_qkw,
                                col_gate=2 * ret_qkw + ret_vw, H=RET_H, K=RET_DK, V=RET_DV,
                                norm_g=ret_norm_g[e], is_gla=False, dec=dec,
                                rope=ret_rope if lat else None, s0=s0r, emit_state=not lat)
                g0 = 2 * ret_qkw + 2 * ret_vw
                go = scan_mixer(z, zr, T=st["T"], col_q=g0, col_k=g0 + gla_kw, col_v=g0 + 2 * gla_kw,
                                col_gate=g0 + 2 * gla_kw + gla_vw, H=GLA_H, K=GLA_DK, V=GLA_DV,
                                norm_g=gla_norm_g[e], is_gla=True, w2=w2p, gbias=gbias,
                                s0=s0g, emit_state=not lat)
                if not lat:
                    ret_states.append(ro[1])
                    gla_states.append(go[1])
                mix.append(matmul_pair(ro[0], go[0], w_out, BF16))
        else:
            o = l // 2
            lam_init = 0.8 - 0.6 * math.exp(-0.3 * l)
            w_in = w_odd_in[o].astype(BF16)
            w_out = w_odd_out[o].astype(BF16)
            ctx = (cache_k[:, o].reshape(Bs, -1, DA_H * 2 * DA_DH).astype(BF16),
                   cache_v[:, o].reshape(Bs, -1, DA_H * 2 * DA_DH).astype(BF16))
            mix = []
            for si, st in enumerate(streams):
                z = matmul(hs[si], w_in, BF16)
                lat = si == 1
                a = diff_attention(z, T=st["T"], lam_p=da_lambda[o], subln_g=da_subln_g[o], lam_init=lam_init,
                                   ctx=ctx if lat else None, rope=da_rope if lat else None)
                if not lat:
                    kv_w = DA_H * 2 * DA_DH
                    ks_out.append(z[:, kv_w:2 * kv_w].astype(F32).reshape(Bp, Tp, DA_H, 2 * DA_DH))
                    vs_out.append(z[:, 2 * kv_w:].astype(F32).reshape(Bp, Tp, DA_H, 2 * DA_DH))
                mix.append(matmul(a, w_out, BF16))

        wf = w_ffn_in[l].reshape(D, 2, F)
        wg = _pad_cols(wf[:, 0], Fp).astype(BF16)
        wv = _pad_cols(wf[:, 1], Fp).astype(BF16)
        cw = _pad_cols(ffn_conv_w[l].reshape(3, 2, F).transpose(1, 0, 2), Fp)
        cb = _pad_cols(ffn_conv_b[l].reshape(2, 1, F), Fp)
        wo = jnp.pad(w_ffn_out[l], ((0, Fp - F), (0, 0))).astype(BF16)
        for si, st in enumerate(streams):
            ys[si], h2 = rowwise(ys[si], mods, st["g0"], st["rpg"],
                                 resid=(mix[si], norm_g[l, 1], l, 2), mod=(norm_g[l, 2], l, 3, 4))
            act = ffn_in(h2, wg, wv, cw, cb, st["T"])
            f = matmul(act, wo, BF16, tk=Fp // 4)
            nxt = (norm_g[l + 1, 0], l + 1, 0, 1) if l + 1 < depth else None
            res = rowwise(ys[si], mods, st["g0"], st["rpg"], resid=(f, norm_g[l, 3], l, 5), mod=nxt)
            ys[si] = res[0]
            hs[si] = res[1] if nxt is not None else None

    new_state_ret = jnp.stack(ret_states, axis=1)
    new_state_gla = jnp.stack(gla_states, axis=1)
    new_cache_k = jnp.stack(ks_out, axis=1)
    new_cache_v = jnp.stack(vs_out, axis=1)
    return (ys[0].reshape(Bp, Tp, D), ys[1].reshape(Bs, Ts, D),
            new_state_ret, new_state_gla, new_cache_k, new_cache_v)
```
